```python
import jax, jax.numpy as jnp
from jax import lax
import numpy as np


D_MODEL = 1024
BATCH = 16
SEQ = 2048
DEPTH = 2

CHUNK = 64
Q_BLOCK = 128
D_MIX = D_MODEL
EPS = 1e-6
N_MOD = 6

GDN_HEADS = 4
GDN_DK = D_MIX // 4 // GDN_HEADS
GDN_DV = D_MIX // 4 // GDN_HEADS
GDN_QK = GDN_HEADS * GDN_DK
GDN_VW = GDN_HEADS * GDN_DV
GDN_CONV = 4

RG_WIDTH = D_MIX // 2
RG_BLOCKS = 8
RG_BLOCK = RG_WIDTH // RG_BLOCKS
RG_CONV = 4
RG_C = 8.0

MLA_HEADS = 4
MLA_NOPE = 64
MLA_ROPE = 32
MLA_V = D_MIX // 4 // MLA_HEADS
MLA_Q_RANK = D_MODEL // 4
MLA_KV_RANK = D_MODEL // 8
ROPE_THETA = 10000.0

D_FF = 4 * D_MODEL

IN_SIZES = (GDN_QK, GDN_QK, GDN_VW, GDN_VW, GDN_HEADS, GDN_HEADS,
            RG_WIDTH, RG_WIDTH,
            MLA_Q_RANK, MLA_KV_RANK, MLA_ROPE)
D_IN = sum(IN_SIZES)

kernel_name = 'hybrid_gdn_rglru_mla_adaln_encoder'


def rmsnorm(x, g):
    xf = x.astype(jnp.float32)
    y = xf * lax.rsqrt(jnp.mean(xf * xf, axis=-1, keepdims=True) + EPS)
    return (y * g.astype(jnp.float32)).astype(x.dtype)


def l2norm(x):
    return x * lax.rsqrt(jnp.sum(x * x, axis=-1, keepdims=True) + EPS)


def causal_depthwise_conv(x, w):
    width = w.shape[0]
    return lax.conv_general_dilated(
        x, w[:, None, :].astype(x.dtype), window_strides=(1,),
        padding=[(width - 1, 0)], dimension_numbers=('NWC', 'WIO', 'NWC'),
        feature_group_count=x.shape[-1])


def gated_delta_rule(q, k, v, g, beta):
    bsz, seq, heads, dk = q.shape
    dv = v.shape[-1]
    n = seq // CHUNK

    def to_chunks(t):
        t = t.reshape((bsz, n, CHUNK) + t.shape[2:])
        return jnp.moveaxis(t, 3, 1)

    q, k, v, g, beta = (to_chunks(t) for t in (q, k, v, g, beta))
    q = q * (dk ** -0.5)
    g = jnp.cumsum(g, axis=-1)
    k_beta = k * beta[..., None]
    v_beta = v * beta[..., None]
    lower_incl = jnp.tril(jnp.ones((CHUNK, CHUNK), bool))
    strict_lower = jnp.tril(jnp.ones((CHUNK, CHUNK), bool), -1)
    diff = g[..., :, None] - g[..., None, :]
    decay = jnp.where(lower_incl, jnp.exp(jnp.where(lower_incl, diff, 0.0)), 0.0)
    lmat = jnp.where(strict_lower, jnp.einsum('bhncd,bhnsd->bhncs', k_beta, k) * decay, 0.0)
    tmat = jnp.eye(CHUNK, dtype=lmat.dtype) + lmat
    u = lax.linalg.triangular_solve(tmat, v_beta, left_side=True, lower=True)
    w = lax.linalg.triangular_solve(tmat, k_beta * jnp.exp(g)[..., None], left_side=True, lower=True)
    qk = jnp.einsum('bhncd,bhnsd->bhncs', q, k) * decay
    q_decay = q * jnp.exp(g)[..., None]
    k_tail = k * jnp.exp(g[..., -1:] - g)[..., None]
    chunk_decay = jnp.exp(g[..., -1])

    def step(state, xs):
        qk_c, qd_c, u_c, w_c, kt_c, cd_c = xs
        v_new = u_c - jnp.einsum('bhcd,bhde->bhce', w_c, state)
        o = jnp.einsum('bhcd,bhde->bhce', qd_c, state) + jnp.einsum('bhcs,bhse->bhce', qk_c, v_new)
        state = state * cd_c[..., None, None] + jnp.einsum('bhcd,bhce->bhde', kt_c, v_new)
        return state, o

    xs = tuple(jnp.moveaxis(t, 2, 0) for t in (qk, q_decay, u, w, k_tail, chunk_decay))
    s0 = jnp.zeros((bsz, heads, dk, dv), jnp.float32)
    _, o = lax.scan(step, s0, xs)
    o = jnp.moveaxis(o, 0, 2)
    return jnp.moveaxis(o, 1, 3).reshape(bsz, seq, heads, dv)


def gdn_mixer(q, k, v, z, a, b, conv_w, a_log, dt_bias, norm_g):
    dtype = q.dtype
    bsz, seq, _ = q.shape
    qkv = jax.nn.silu(causal_depthwise_conv(jnp.concatenate([q, k, v], axis=-1), conv_w))
    qkv = qkv.astype(jnp.float32)
    q, k, v = jnp.split(qkv, [GDN_QK, 2 * GDN_QK], axis=-1)
    q = l2norm(q.reshape(bsz, seq, GDN_HEADS, GDN_DK))
    k = l2norm(k.reshape(bsz, seq, GDN_HEADS, GDN_DK))
    v = v.reshape(bsz, seq, GDN_HEADS, GDN_DV)
    g = -jnp.exp(a_log.astype(jnp.float32)) * jax.nn.softplus(a.astype(jnp.float32) + dt_bias.astype(jnp.float32))
    beta = jax.nn.sigmoid(b.astype(jnp.float32))
    o = gated_delta_rule(q, k, v, g, beta)
    zg = jax.nn.silu(z.astype(jnp.float32).reshape(bsz, seq, GDN_HEADS, GDN_DV))
    o = rmsnorm(o, norm_g) * zg
    return o.reshape(bsz, seq, GDN_VW).astype(dtype)


def rglru_mixer(xb, gate, conv_w, conv_b, w_a, b_a, w_x, b_x, lam):
    dtype = xb.dtype
    bsz, seq, _ = xb.shape
    xc = causal_depthwise_conv(xb, conv_w) + conv_b
    xblk = xc.reshape(bsz, seq, RG_BLOCKS, RG_BLOCK)
    r = jax.nn.sigmoid(jnp.einsum('bsgi,gij->bsgj', xblk, w_a).reshape(bsz, seq, RG_WIDTH) + b_a)
    i = jax.nn.sigmoid(jnp.einsum('bsgi,gij->bsgj', xblk, w_x).reshape(bsz, seq, RG_WIDTH) + b_x)
    log_a = -RG_C * r.astype(jnp.float32) * jax.nn.softplus(-lam.astype(jnp.float32))
    a = jnp.exp(log_a)
    mult = jnp.sqrt(-jnp.expm1(2.0 * log_a))
    bterm = mult * (i * xc).astype(jnp.float32)

    def combine(left, right):
        a1, b1 = left
        a2, b2 = right
        return a1 * a2, a2 * b1 + b2

    _, h = lax.associative_scan(combine, (a, bterm), axis=1)
    return (h * jax.nn.gelu(gate.astype(jnp.float32))).astype(dtype)


def rope(x, cos, sin):
    x1, x2 = jnp.split(x, 2, axis=-1)
    return jnp.concatenate([x1 * cos - x2 * sin, x2 * cos + x1 * sin], axis=-1)


def mla_mixer(q_lat, kv_lat, k_rope, positions, q_norm_g, w_qb, kv_norm_g, w_kvb):
    dtype = q_lat.dtype
    bsz, seq, _ = q_lat.shape
    q = (rmsnorm(q_lat, q_norm_g) @ w_qb).reshape(bsz, seq, MLA_HEADS, MLA_NOPE + MLA_ROPE)
    q_nope, q_pe = jnp.split(q.astype(jnp.float32), [MLA_NOPE], axis=-1)
    kv = (rmsnorm(kv_lat, kv_norm_g) @ w_kvb).reshape(bsz, seq, MLA_HEADS, MLA_NOPE + MLA_V)
    k_nope, v = jnp.split(kv.astype(jnp.float32), [MLA_NOPE], axis=-1)
    inv_freq = ROPE_THETA ** (-jnp.arange(0, MLA_ROPE, 2, dtype=jnp.float32) / MLA_ROPE)
    ang = positions.astype(jnp.float32)[..., None] * inv_freq
    cos = jnp.cos(ang)[:, :, None, :]
    sin = jnp.sin(ang)[:, :, None, :]
    q_pe = rope(q_pe, cos, sin)
    k_pe = rope(k_rope.astype(jnp.float32)[:, :, None, :], cos, sin)[:, :, 0]
    scale = (MLA_NOPE + MLA_ROPE) ** -0.5
    chunk_id = jnp.arange(seq) // CHUNK
    outs = []
    for blk in range(seq // Q_BLOCK):
        q0, q1 = blk * Q_BLOCK, (blk + 1) * Q_BLOCK
        s = (jnp.einsum('bqhd,bkhd->bhqk', q_nope[:, q0:q1], k_nope[:, :q1])
             + jnp.einsum('bqhr,bkr->bhqk', q_pe[:, q0:q1], k_pe[:, :q1])) * scale
        mask = chunk_id[None, :q1] <= chunk_id[q0:q1, None]
        p = jax.nn.softmax(jnp.where(mask, s, -jnp.inf), axis=-1)
        outs.append(jnp.einsum('bhqk,bkhd->bqhd', p, v[:, :q1]))
    o = jnp.concatenate(outs, axis=1)
    return o.reshape(bsz, seq, MLA_HEADS * MLA_V).astype(dtype)


def setup_inputs(seed: int = 0) -> dict:
    key = jax.random.key(seed)
    ks = jax.random.split(key, 32)
    f32 = jnp.float32
    nrm = lambda k, shape, s: jax.random.normal(k, shape, f32) * s
    x = jax.random.normal(ks[0], (BATCH, SEQ, D_MODEL), f32)
    c = jax.random.normal(ks[1], (BATCH, D_MODEL), f32)
    offsets = jax.random.randint(ks[2], (BATCH, 1), 0, 4096, dtype=jnp.int32)
    positions = offsets + jnp.arange(SEQ, dtype=jnp.int32)[None, :]
    w_mod = nrm(ks[3], (DEPTH, D_MODEL, N_MOD * D_MODEL), 0.5 * D_MODEL ** -0.5)
    b_mod = nrm(ks[4], (DEPTH, N_MOD * D_MODEL), 0.02)
    norm_mix_g = 1.0 + nrm(ks[5], (DEPTH, D_MODEL), 0.02)
    w_in = nrm(ks[6], (DEPTH, D_MODEL, D_IN), D_MODEL ** -0.5)
    gdn_conv_w = nrm(ks[7], (DEPTH, GDN_CONV, 2 * GDN_QK + GDN_VW), GDN_CONV ** -0.5)
    gdn_a_log = jnp.log(jax.random.uniform(ks[8], (DEPTH, GDN_HEADS), f32, 1.0, 16.0))
    dt = jnp.exp(jax.random.uniform(ks[9], (DEPTH, GDN_HEADS), f32, np.log(1e-3), np.log(1e-1)))
    gdn_dt_bias = dt + jnp.log(-jnp.expm1(-dt))
    gdn_norm_g = 1.0 + nrm(ks[10], (DEPTH, GDN_DV), 0.02)
    rg_conv_w = nrm(ks[11], (DEPTH, RG_CONV, RG_WIDTH), RG_CONV ** -0.5)
    rg_conv_b = nrm(ks[12], (DEPTH, RG_WIDTH), 0.02)
    rg_w_a = nrm(ks[13], (DEPTH, RG_BLOCKS, RG_BLOCK, RG_BLOCK), RG_BLOCK ** -0.5)
    rg_b_a = nrm(ks[14], (DEPTH, RG_WIDTH), 0.02)
    rg_w_x = nrm(ks[15], (DEPTH, RG_BLOCKS, RG_BLOCK, RG_BLOCK), RG_BLOCK ** -0.5)
    rg_b_x = nrm(ks[16], (DEPTH, RG_WIDTH), 0.02)
    a0 = jax.random.uniform(ks[17], (DEPTH, RG_WIDTH), f32, 0.9, 0.999) ** (1.0 / RG_C)
    rg_lambda = jnp.log(a0) - jnp.log1p(-a0)
    mla_q_norm_g = 1.0 + nrm(ks[18], (DEPTH, MLA_Q_RANK), 0.02)
    mla_w_qb = nrm(ks[19], (DEPTH, MLA_Q_RANK, MLA_HEADS * (MLA_NOPE + MLA_ROPE)), MLA_Q_RANK ** -0.5)
    mla_kv_norm_g = 1.0 + nrm(ks[20], (DEPTH, MLA_KV_RANK), 0.02)
    mla_w_kvb = nrm(ks[21], (DEPTH, MLA_KV_RANK, MLA_HEADS * (MLA_NOPE + MLA_V)), MLA_KV_RANK ** -0.5)
    w_out = nrm(ks[22], (DEPTH, D_MIX, D_MODEL), D_MIX ** -0.5)
    norm_mlp_g = 1.0 + nrm(ks[23], (DEPTH, D_MODEL), 0.02)
    w_mlp_in = nrm(ks[24], (DEPTH, D_MODEL, D_FF), D_MODEL ** -0.5)
    w_mlp_out = nrm(ks[25], (DEPTH, D_FF, D_MODEL), D_FF ** -0.5)
    final_norm_g = 1.0 + nrm(ks[26], (D_MODEL,), 0.02)
    return {'x': x, 'c': c, 'positions': positions, 'w_mod': w_mod, 'b_mod': b_mod,
            'norm_mix_g': norm_mix_g, 'w_in': w_in, 'gdn_conv_w': gdn_conv_w,
            'gdn_a_log': gdn_a_log, 'gdn_dt_bias': gdn_dt_bias, 'gdn_norm_g': gdn_norm_g,
            'rg_conv_w': rg_conv_w, 'rg_conv_b': rg_conv_b, 'rg_w_a': rg_w_a, 'rg_b_a': rg_b_a,
            'rg_w_x': rg_w_x, 'rg_b_x': rg_b_x, 'rg_lambda': rg_lambda,
            'mla_q_norm_g': mla_q_norm_g, 'mla_w_qb': mla_w_qb, 'mla_kv_norm_g': mla_kv_norm_g,
            'mla_w_kvb': mla_w_kvb, 'w_out': w_out, 'norm_mlp_g': norm_mlp_g,
            'w_mlp_in': w_mlp_in, 'w_mlp_out': w_mlp_out, 'final_norm_g': final_norm_g}


def reference(x, c, positions, w_mod, b_mod, norm_mix_g, w_in, gdn_conv_w, gdn_a_log, gdn_dt_bias,
              gdn_norm_g, rg_conv_w, rg_conv_b, rg_w_a, rg_b_a, rg_w_x, rg_b_x, rg_lambda,
              mla_q_norm_g, mla_w_qb, mla_kv_norm_g, mla_w_kvb, w_out, norm_mlp_g,
              w_mlp_in, w_mlp_out, final_norm_g):
    split_points = np.cumsum(IN_SIZES)[:-1].tolist()
    h = x
    c_act = jax.nn.silu(c)
    for l in range(DEPTH):
        mod = c_act @ w_mod[l] + b_mod[l]
        sh_m, sc_m, gt_m, sh_f, sc_f, gt_f = (m[:, None, :] for m in jnp.split(mod, N_MOD, axis=-1))
        u = rmsnorm(h, norm_mix_g[l]) * (1.0 + sc_m) + sh_m
        proj = u @ w_in[l]
        gq, gk, gv, gz, ga, gb, rx, rgate, mq, mkv, mkr = jnp.split(proj, split_points, axis=-1)
        o_a = gdn_mixer(gq, gk, gv, gz, ga, gb, gdn_conv_w[l], gdn_a_log[l], gdn_dt_bias[l], gdn_norm_g[l])
        o_b = rglru_mixer(rx, rgate, rg_conv_w[l], rg_conv_b[l], rg_w_a[l], rg_b_a[l],
                          rg_w_x[l], rg_b_x[l], rg_lambda[l])
        o_c = mla_mixer(mq, mkv, mkr, positions, mla_q_norm_g[l], mla_w_qb[l],
                        mla_kv_norm_g[l], mla_w_kvb[l])
        mix = jnp.concatenate([o_a, o_b, o_c], axis=-1) @ w_out[l]
        h = h + gt_m * mix
        u = rmsnorm(h, norm_mlp_g[l]) * (1.0 + sc_f) + sh_f
        f = jnp.square(jax.nn.relu(u @ w_mlp_in[l])) @ w_mlp_out[l]
        h = h + gt_f * f
    return rmsnorm(h, final_norm_g)
```

```python
import functools

import jax
import jax.numpy as jnp
from jax import lax
from jax.experimental import pallas as pl
from jax.experimental.pallas import tpu as pltpu

F32 = jnp.float32
BF16 = jnp.bfloat16

EPS = 1e-6
N_MOD = 6
CHUNK = 64
GDN_HEADS = 4
GDN_DK = 64
GDN_W = GDN_HEADS * GDN_DK
GDN_CONV = 4
RG_WIDTH = 512
RG_BLOCK = 64
RG_CONV = 4
RG_C = 8.0
MLA_HEADS = 4
MLA_NOPE = 64
MLA_ROPE = 32
MLA_V = 64
MLA_Q_RANK = 256
MLA_KV_RANK = 128
ROPE_THETA = 10000.0
LANES = 128
SUBLANES = 8
HEAD_PAD = LANES

COL_GDN = 0
COL_RX = 1024
COL_RGATE = 1536
COL_MQ = 2048
COL_MKV = 2304
COL_MISC = 2432
NP_IN = 2560
MISC_A = MLA_ROPE
MISC_B = MLA_ROPE + GDN_HEADS

SLAB = 256
VMEM_LIMIT = 56 * 1024 * 1024


def _cparams(n_axes):
    return pltpu.CompilerParams(dimension_semantics=("arbitrary",) * n_axes,
                                vmem_limit_bytes=VMEM_LIMIT)


def _dot(a, b):
    return jnp.dot(a, b, preferred_element_type=F32)


def _dot_nt(a, b):
    return lax.dot_general(a, b, (((1,), (1,)), ((), ())), preferred_element_type=F32)


def _split3(x):
    hi = x.astype(BF16)
    r1 = x - hi.astype(F32)
    mid = r1.astype(BF16)
    lo = (r1 - mid.astype(F32)).astype(BF16)
    return hi, mid, lo


def _dot3_lhs(x, w):
    hi, mid, lo = _split3(x)
    return _dot(hi, w) + _dot(mid, w) + _dot(lo, w)


def _dot3_rhs(w, x):
    hi, mid, lo = _split3(x)
    return _dot(w, hi) + _dot(w, mid) + _dot(w, lo)


def _sigmoid(x):
    return 1.0 / (1.0 + jnp.exp(-x))


def _silu(x):
    return x * _sigmoid(x)


def _softplus(x):
    return jnp.maximum(x, 0.0) + jnp.log1p(jnp.exp(-jnp.abs(x)))


def _causal_conv(ref, r0, rows, col0, ncols, w):
    taps = w.shape[0]
    cur = ref[0, pl.ds(r0, rows), col0:col0 + ncols]
    pstart = pl.multiple_of(jnp.maximum(r0 - SUBLANES, 0), SUBLANES)
    prev = ref[0, pl.ds(pstart, SUBLANES), col0:col0 + ncols]
    prev = jnp.where(r0 > 0, prev, 0.0)
    xcat = jnp.concatenate([prev, cur], axis=0)
    acc = cur * w[taps - 1:taps, :]
    for j in range(taps - 1):
        shifted = pltpu.roll(xcat, taps - 1 - j, axis=0)[SUBLANES:, :]
        acc = acc + shifted * w[j:j + 1, :]
    return acc


def _mod_kernel(c_ref, w_ref, b_ref, o_ref):
    ca = _silu(c_ref[...])
    o_ref[0] = _dot(ca.astype(BF16), w_ref[0].astype(BF16)) + b_ref[0]


def _modulation(c, w_mod, b_mod):
    depth, d, n = w_mod.shape
    bsz = c.shape[0]
    tn = 1536
    return pl.pallas_call(
        _mod_kernel,
        grid=(depth, n // tn),
        in_specs=[pl.BlockSpec((bsz, d), lambda l, j: (0, 0)),
                  pl.BlockSpec((1, d, tn), lambda l, j: (l, 0, j)),
                  pl.BlockSpec((1, 1, tn), lambda l, j: (l, 0, j))],
        out_specs=pl.BlockSpec((1, bsz, tn), lambda l, j: (l, 0, j)),
        out_shape=jax.ShapeDtypeStruct((depth, bsz, n), F32),
        compiler_params=_cparams(2),
        name="adaln_mod",
    )(c, w_mod, b_mod.reshape(depth, 1, n))


def _modnorm(x, g, sc, sh):
    ms = jnp.mean(x * x, axis=-1, keepdims=True)
    return (x * lax.rsqrt(ms + EPS) * g) * (1.0 + sc) + sh


def _inproj_kernel(h_ref, sh_ref, sc_ref, g_ref, w_ref, o_ref):
    u = _modnorm(h_ref[0], g_ref[...], sc_ref[0], sh_ref[0])
    o_ref[0] = _dot(u.astype(BF16), w_ref[...])


def _in_projection(h, mod_l, g, w_pad, tm=512):
    bsz, seq, d = h.shape
    npad = w_pad.shape[1]
    mod3 = mod_l.reshape(bsz, 1, N_MOD * d)
    return pl.pallas_call(
        _inproj_kernel,
        grid=(bsz, seq // tm),
        in_specs=[pl.BlockSpec((1, tm, d), lambda b, i: (b, i, 0)),
                  pl.BlockSpec((1, 1, d), lambda b, i: (b, 0, 0)),
                  pl.BlockSpec((1, 1, d), lambda b, i: (b, 0, 1)),
                  pl.BlockSpec((1, d), lambda b, i: (0, 0)),
                  pl.BlockSpec((d, npad), lambda b, i: (0, 0))],
        out_specs=pl.BlockSpec((1, tm, npad), lambda b, i: (b, i, 0)),
        out_shape=jax.ShapeDtypeStruct((bsz, seq, npad), F32),
        compiler_params=_cparams(2),
        name="norm_inproj",
    )(h, mod3, mod3, g.reshape(1, d), w_pad)


def _head_of(idx):
    return idx // GDN_DK


def _blockdiag(x):
    xb = x.astype(BF16)
    lane_head = _head_of(lax.broadcasted_iota(jnp.int32, xb.shape, 1))
    zero = jnp.zeros_like(xb)
    return jnp.concatenate([jnp.where(lane_head == hh, xb, zero) for hh in range(GDN_HEADS)], axis=0)


def _gdn_kernel(qkvz_ref, misc_ref, convw_ref, nalog_ref, dtb_ref, ng_ref, o_ref,
                u_s, w_s, qd_s, qk_s, ktt_s, cd_s, o_s, st_s):
    seq = qkvz_ref.shape[1]
    n_slab = seq // SLAB
    n_chunk = seq // CHUNK
    per_slab = SLAB // CHUNK

    ii = lax.broadcasted_iota(jnp.int32, (GDN_W, GDN_W), 0)
    jj = lax.broadcasted_iota(jnp.int32, (GDN_W, GDN_W), 1)
    same_blk = _head_of(ii) == _head_of(jj)
    ones_bd = jnp.where(same_blk, 1.0, 0.0).astype(BF16)
    tri_bd = jnp.where(same_blk & (ii >= jj), 1.0, 0.0).astype(BF16)
    mi = lax.broadcasted_iota(jnp.int32, (LANES, GDN_W), 0)
    mj = lax.broadcasted_iota(jnp.int32, (LANES, GDN_W), 1)
    sel_a = jnp.where(mi == MISC_A + _head_of(mj), 1.0, 0.0).astype(BF16)
    sel_b = jnp.where(mi == MISC_B + _head_of(mj), 1.0, 0.0).astype(BF16)
    ci = lax.broadcasted_iota(jnp.int32, (CHUNK, GDN_W), 0)
    cj = lax.broadcasted_iota(jnp.int32, (CHUNK, GDN_W), 1)
    cs = cj % CHUNK
    lower_incl = ci >= cs
    strict_lower = ci > cs
    eye_hl = jnp.where(ci == cs, 1.0, 0.0)
    first_lane = cs == 0
    ones_lhs = jnp.ones((CHUNK, GDN_W), BF16)
    convw = convw_ref[...]

    def slab_body(si, carry):
        r0 = pl.multiple_of(si * SLAB, SLAB)
        y = _silu(_causal_conv(qkvz_ref, r0, SLAB, 0, 3 * GDN_W, convw))
        q, k, v = y[:, :GDN_W], y[:, GDN_W:2 * GDN_W], y[:, 2 * GDN_W:]

        def l2n(x):
            xx = x * x
            hi = xx.astype(BF16)
            lo = (xx - hi.astype(F32)).astype(BF16)
            ss = _dot(hi, ones_bd) + _dot(lo, ones_bd)
            return x * lax.rsqrt(ss + EPS)

        q = l2n(q) * (GDN_DK ** -0.5)
        k = l2n(k)
        misc = misc_ref[0, pl.ds(r0, SLAB), :]
        a_exp = _dot3_lhs(misc, sel_a)
        b_exp = _dot3_lhs(misc, sel_b)
        g = nalog_ref[...] * _softplus(a_exp + dtb_ref[...])
        beta = _sigmoid(b_exp)
        gcum = _dot3_rhs(tri_bd, g)
        glast = _dot3_rhs(ones_bd, g)
        eg = jnp.exp(gcum)
        kb = k * beta
        vb = v * beta
        kbg = kb * eg
        qd_s[pl.ds(r0, SLAB), :] = (q * eg).astype(BF16)
        kt = k * jnp.exp(glast - gcum)
        cdec = jnp.exp(glast)

        for c in range(per_slab):
            lo_r, hi_r = c * CHUNK, (c + 1) * CHUNK
            row0 = pl.multiple_of(r0 + lo_r, CHUNK)
            ci_idx = si * per_slab + c
            kc, kbc, qc, gc = k[lo_r:hi_r], kb[lo_r:hi_r], q[lo_r:hi_r], gcum[lo_r:hi_r]
            bdk = _blockdiag(kc)
            aq = _dot_nt(jnp.concatenate([kbc, qc], axis=0).astype(BF16), bdk)
            gr = jnp.zeros((CHUNK, GDN_W), F32)
            for piece in _split3(jnp.where(first_lane, gc, 0.0)):
                lane_head = _head_of(lax.broadcasted_iota(jnp.int32, piece.shape, 1))
                bdp = jnp.concatenate([jnp.where(lane_head == hh, piece, jnp.zeros_like(piece))
                                       for hh in range(GDN_HEADS)], axis=0)
                gr = gr + _dot_nt(ones_lhs, bdp)
            decay = jnp.exp(jnp.where(lower_incl, gc - gr, 0.0))
            lmat = jnp.where(strict_lower, aq[:CHUNK] * decay, 0.0)
            qk_s[pl.ds(row0, CHUNK), :] = jnp.where(lower_incl, aq[CHUNK:] * decay, 0.0).astype(BF16)
            blk = 1
            dinv = eye_hl - jnp.where((ci // 2 == cs // 2), lmat, 0.0)
            blk = 2
            while blk < CHUNK:
                sub = jnp.where((ci // (2 * blk) == cs // (2 * blk)) & (ci // blk != cs // blk), lmat, 0.0)
                p = _dot(dinv.astype(BF16), _blockdiag(sub))
                dinv = dinv - _dot(p.astype(BF16), _blockdiag(dinv))
                blk *= 2
            tb = dinv.astype(BF16)
            u_s[pl.ds(row0, CHUNK), :] = _dot(tb, _blockdiag(vb[lo_r:hi_r]))
            w_s[pl.ds(row0, CHUNK), :] = _dot(tb, _blockdiag(kbg[lo_r:hi_r])).astype(BF16)
            ktc = jnp.concatenate([kt[lo_r:hi_r], jnp.zeros((LANES - CHUNK, GDN_W), F32)], axis=0)
            ktt_s[ci_idx] = ktc.T.astype(BF16)
            cd_s[ci_idx] = cdec[lo_r:lo_r + SUBLANES]
        return carry

    lax.fori_loop(0, n_slab, slab_body, 0)

    st_s[...] = jnp.zeros_like(st_s)

    def chunk_body(ci_idx, carry):
        row0 = pl.multiple_of(ci_idx * CHUNK, CHUNK)
        sb = st_s[...].astype(BF16)
        lhs = jnp.concatenate([w_s[pl.ds(row0, CHUNK), :], qd_s[pl.ds(row0, CHUNK), :]], axis=0)
        wq = _dot(lhs, sb)
        v_new = u_s[pl.ds(row0, CHUNK), :] - wq[:CHUNK]
        o_s[pl.ds(row0, CHUNK), :] = wq[CHUNK:] + _dot(qk_s[pl.ds(row0, CHUNK), :], _blockdiag(v_new))
        vpad = jnp.concatenate([v_new.astype(BF16), jnp.zeros((LANES - CHUNK, GDN_W), BF16)], axis=0)
        upd = _dot(ktt_s[ci_idx], vpad)
        st_s[...] = st_s[...] * cd_s[ci_idx][0:1, :] + jnp.where(same_blk, upd, 0.0)
        return carry

    lax.fori_loop(0, n_chunk, chunk_body, 0)

    mean_bd = jnp.where(same_blk, 1.0 / GDN_DK, 0.0).astype(BF16)

    def out_body(si, carry):
        r0 = pl.multiple_of(si * SLAB, SLAB)
        o = o_s[pl.ds(r0, SLAB), :]
        oo = o * o
        hi = oo.astype(BF16)
        lo = (oo - hi.astype(F32)).astype(BF16)
        ms = _dot(hi, mean_bd) + _dot(lo, mean_bd)
        y = o * lax.rsqrt(ms + EPS) * ng_ref[...]
        z = qkvz_ref[0, pl.ds(r0, SLAB), 3 * GDN_W:4 * GDN_W]
        o_ref[0, pl.ds(r0, SLAB), :] = (y * _silu(z)).astype(o_ref.dtype)
        return carry

    lax.fori_loop(0, n_slab, out_body, 0)


def _gdn_mixer(proj, conv_w, a_log, dt_bias, norm_g):
    bsz, seq, _ = proj.shape
    n_chunk = seq // CHUNK
    nalog = jnp.repeat(-jnp.exp(a_log.astype(F32)), GDN_DK).reshape(1, GDN_W)
    dtb = jnp.repeat(dt_bias.astype(F32), GDN_DK).reshape(1, GDN_W)
    ng = jnp.tile(norm_g.astype(F32), GDN_HEADS).reshape(1, GDN_W)
    row = lambda shape: pl.BlockSpec(shape, lambda b: (0,) * len(shape))
    return pl.pallas_call(
        _gdn_kernel,
        grid=(bsz,),
        in_specs=[pl.BlockSpec((1, seq, 4 * GDN_W), lambda b: (b, 0, COL_GDN // (4 * GDN_W))),
                  pl.BlockSpec((1, seq, LANES), lambda b: (b, 0, COL_MISC // LANES)),
                  row((GDN_CONV, 3 * GDN_W)), row((1, GDN_W)), row((1, GDN_W)), row((1, GDN_W))],
        out_specs=pl.BlockSpec((1, seq, GDN_W), lambda b: (b, 0, 0)),
        out_shape=jax.ShapeDtypeStruct((bsz, seq, GDN_W), BF16),
        scratch_shapes=[pltpu.VMEM((seq, GDN_W), F32),
                        pltpu.VMEM((seq, GDN_W), BF16),
                        pltpu.VMEM((seq, GDN_W), BF16),
                        pltpu.VMEM((seq, GDN_W), BF16),
                        pltpu.VMEM((n_chunk, GDN_W, LANES), BF16),
                        pltpu.VMEM((n_chunk, SUBLANES, GDN_W), F32),
                        pltpu.VMEM((seq, GDN_W), F32),
                        pltpu.VMEM((GDN_W, GDN_W), F32)],
        compiler_params=_cparams(1),
        name="gdn_mixer",
    )(proj, proj, conv_w, nalog, dtb, ng)


def _gelu_tanh(x):
    return 0.5 * x * (1.0 + jnp.tanh(0.7978845608028654 * (x + 0.044715 * (x * x * x))))


def _rg_kernel(rx_ref, gate_ref, convw_ref, convb_ref, w_ref, ba_ref, bx_ref, lam_ref, o_ref, hc_s):
    seq = rx_ref.shape[1]
    n_slab = seq // SLAB
    n_tile = SLAB // SUBLANES
    hc_s[...] = jnp.zeros_like(hc_s)
    convw = convw_ref[...]
    sub = lax.broadcasted_iota(jnp.int32, (n_tile, SUBLANES, LANES), 1)

    def slab_body(si, carry):
        r0 = pl.multiple_of(si * SLAB, SLAB)
        xc_all = _causal_conv(rx_ref, r0, SLAB, 0, RG_WIDTH, convw) + convb_ref[...]
        for gi in range(RG_WIDTH // LANES):
            l0, l1 = gi * LANES, (gi + 1) * LANES
            xc = xc_all[:, l0:l1]
            pre = _dot(xc.astype(BF16), w_ref[gi])
            r = _sigmoid(pre[:, :LANES] + ba_ref[:, l0:l1])
            ig = _sigmoid(pre[:, LANES:] + bx_ref[:, l0:l1])
            log_a = (-RG_C) * r * _softplus(-lam_ref[:, l0:l1])
            a = jnp.exp(log_a)
            mult = jnp.sqrt(1.0 - a * a)
            bt = mult * (ig * xc)
            a3 = a.reshape(n_tile, SUBLANES, LANES)
            b3 = bt.reshape(n_tile, SUBLANES, LANES)
            d = 1
            while d < SUBLANES:
                a_sh = pltpu.roll(a3, d, axis=1)
                b_sh = pltpu.roll(b3, d, axis=1)
                keep = sub >= d
                b3 = jnp.where(keep, a3 * b_sh + b3, b3)
                a3 = jnp.where(keep, a3 * a_sh, a3)
                d *= 2
            car = hc_s[:, l0:l1]
            hs = []
            for n in range(n_tile):
                hn = a3[n] * car + b3[n]
                hs.append(hn)
                car = jnp.broadcast_to(hn[SUBLANES - 1:SUBLANES, :], (SUBLANES, LANES))
            hc_s[:, l0:l1] = car
            hh = jnp.concatenate(hs, axis=0)
            gate = gate_ref[0, pl.ds(r0, SLAB), l0:l1]
            o_ref[0, pl.ds(r0, SLAB), l0:l1] = (hh * _gelu_tanh(gate)).astype(o_ref.dtype)
        return carry

    lax.fori_loop(0, n_slab, slab_body, 0)


def _rg_weights(w_a, w_x):
    def pair_bd(w):
        z = jnp.zeros((RG_BLOCK, RG_BLOCK), w.dtype)
        out = []
        for gi in range(RG_WIDTH // LANES):
            top = jnp.concatenate([w[2 * gi], z], axis=1)
            bot = jnp.concatenate([z, w[2 * gi + 1]], axis=1)
            out.append(jnp.concatenate([top, bot], axis=0))
        return jnp.stack(out)
    return jnp.concatenate([pair_bd(w_a), pair_bd(w_x)], axis=2).astype(BF16)


def _rg_mixer(proj, conv_w, conv_b, w_a, b_a, w_x, b_x, lam):
    bsz, seq, _ = proj.shape
    row = lambda shape: pl.BlockSpec(shape, lambda b: (0,) * len(shape))
    r2 = lambda v: v.astype(F32).reshape(1, RG_WIDTH)
    return pl.pallas_call(
        _rg_kernel,
        grid=(bsz,),
        in_specs=[pl.BlockSpec((1, seq, RG_WIDTH), lambda b: (b, 0, COL_RX // RG_WIDTH)),
                  pl.BlockSpec((1, seq, RG_WIDTH), lambda b: (b, 0, COL_RGATE // RG_WIDTH)),
                  row((RG_CONV, RG_WIDTH)), row((1, RG_WIDTH)),
                  row((RG_WIDTH // LANES, LANES, 2 * LANES)),
                  row((1, RG_WIDTH)), row((1, RG_WIDTH)), row((1, RG_WIDTH))],
        out_specs=pl.BlockSpec((1, seq, RG_WIDTH), lambda b: (b, 0, 0)),
        out_shape=jax.ShapeDtypeStruct((bsz, seq, RG_WIDTH), BF16),
        scratch_shapes=[pltpu.VMEM((SUBLANES, RG_WIDTH), F32)],
        compiler_params=_cparams(1),
        name="rglru_mixer",
    )(proj, proj, conv_w, r2(conv_b), _rg_weights(w_a, w_x), r2(b_a), r2(b_x), r2(lam))


MLA_TQ = 256
PE0 = MLA_NOPE
PE_HALF = MLA_ROPE // 2


def _rms(x, g):
    ms = jnp.mean(x * x, axis=-1, keepdims=True)
    return x * lax.rsqrt(ms + EPS) * g


def _mla_kernel(mq_ref, mkv_ref, misc_ref, pos_ref, invf_ref, qg_ref, kvg_ref, wq_ref, wk_ref, wv_ref,
                o_ref, q_s, k_s, v_s):
    seq = mq_ref.shape[1]
    n_slab = seq // SLAB
    n_qblk = seq // MLA_TQ
    lane = lax.broadcasted_iota(jnp.int32, (SLAB, LANES), 1)
    x1_mask = (lane >= PE0) & (lane < PE0 + PE_HALF)
    x2_mask = (lane >= PE0 + PE_HALF) & (lane < PE0 + MLA_ROPE)
    scale = (MLA_NOPE + MLA_ROPE) ** -0.5

    def prep_body(si, carry):
        r0 = pl.multiple_of(si * SLAB, SLAB)
        ang = pos_ref[0, pl.ds(r0, SLAB), :].astype(F32) * invf_ref[...]
        cos = jnp.cos(ang)
        sin = jnp.sin(ang)
        c_neg = jnp.where(x1_mask, -sin, 0.0)
        c_pos = jnp.where(x2_mask, sin, 0.0)

        def rope(t):
            return t * cos + pltpu.roll(t, LANES - PE_HALF, axis=1) * c_neg + pltpu.roll(t, PE_HALF, axis=1) * c_pos

        qn = _rms(mq_ref[0, pl.ds(r0, SLAB), :], qg_ref[...]).astype(BF16)
        q = _dot(qn, wq_ref[...])
        kvn = _rms(mkv_ref[0, pl.ds(r0, SLAB), :], kvg_ref[...]).astype(BF16)
        kn = _dot(kvn, wk_ref[...])
        v_s[pl.ds(r0, SLAB), :] = _dot(kvn, wv_ref[...]).astype(BF16)
        kr = pltpu.roll(misc_ref[0, pl.ds(r0, SLAB), :], PE0, axis=1)
        kr = rope(jnp.where(x1_mask | x2_mask, kr, 0.0))
        for hh in range(MLA_HEADS):
            l0, l1 = hh * HEAD_PAD, (hh + 1) * HEAD_PAD
            q_s[pl.ds(r0, SLAB), l0:l1] = (rope(q[:, l0:l1]) * scale).astype(BF16)
            k_s[pl.ds(r0, SLAB), l0:l1] = (kn[:, l0:l1] + kr).astype(BF16)
        return carry

    lax.fori_loop(0, n_slab, prep_body, 0)

    ri = lax.broadcasted_iota(jnp.int32, (MLA_TQ, MLA_TQ), 0)
    ki = lax.broadcasted_iota(jnp.int32, (MLA_TQ, MLA_TQ), 1)
    diag_mask = (ki // CHUNK) <= (ri // CHUNK)

    def q_body(qi, carry):
        q0 = pl.multiple_of(qi * MLA_TQ, MLA_TQ)
        outs = []
        for hh in range(MLA_HEADS):
            l0, l1 = hh * HEAD_PAD, (hh + 1) * HEAD_PAD
            qh = q_s[pl.ds(q0, MLA_TQ), l0:l1]

            def attend(k0, state, masked):
                m, l, acc = state
                s = _dot_nt(qh, k_s[pl.ds(k0, MLA_TQ), l0:l1])
                if masked:
                    s = jnp.where(diag_mask, s, -jnp.inf)
                m_new = jnp.maximum(m, jnp.max(s, axis=-1, keepdims=True))
                alpha = jnp.exp(m - m_new)
                p = jnp.exp(s - m_new)
                l = alpha * l + jnp.sum(p, axis=-1, keepdims=True)
                acc = alpha * acc + _dot(p.astype(BF16), v_s[pl.ds(k0, MLA_TQ), l0:l1])
                return m_new, l, acc

            init = (jnp.full((MLA_TQ, 1), -jnp.inf, F32), jnp.zeros((MLA_TQ, 1), F32),
                    jnp.zeros((MLA_TQ, HEAD_PAD), F32))
            state = attend(q0, init, True)
            state = lax.fori_loop(
                0, qi, lambda kj, st: attend(pl.multiple_of(kj * MLA_TQ, MLA_TQ), st, False), state)
            _, l, acc = state
            outs.append(acc / l)
        for pair in range(MLA_HEADS // 2):
            packed = outs[2 * pair] + pltpu.roll(outs[2 * pair + 1], MLA_V, axis=1)
            o_ref[0, pl.ds(q0, MLA_TQ), pair * LANES:(pair + 1) * LANES] = packed.astype(o_ref.dtype)
        return carry

    lax.fori_loop(0, n_qblk, q_body, 0)


def _mla_weights(w_qb, w_kvb):
    qr = w_qb.reshape(MLA_Q_RANK, MLA_HEADS, MLA_NOPE + MLA_ROPE)
    zq = jnp.zeros((MLA_Q_RANK, MLA_HEADS, HEAD_PAD - MLA_NOPE - MLA_ROPE), w_qb.dtype)
    wq = jnp.concatenate([qr, zq], axis=2).reshape(MLA_Q_RANK, MLA_HEADS * HEAD_PAD)
    kvr = w_kvb.reshape(MLA_KV_RANK, MLA_HEADS, MLA_NOPE + MLA_V)
    zk = jnp.zeros((MLA_KV_RANK, MLA_HEADS, HEAD_PAD - MLA_NOPE), w_kvb.dtype)
    wk = jnp.concatenate([kvr[:, :, :MLA_NOPE], zk], axis=2).reshape(MLA_KV_RANK, MLA_HEADS * HEAD_PAD)
    zv = jnp.zeros((MLA_KV_RANK, MLA_HEADS, HEAD_PAD - MLA_V), w_kvb.dtype)
    wv = jnp.concatenate([kvr[:, :, MLA_NOPE:], zv], axis=2).reshape(MLA_KV_RANK, MLA_HEADS * HEAD_PAD)
    return wq.astype(BF16), wk.astype(BF16), wv.astype(BF16)


def _mla_mixer(proj, positions, q_norm_g, w_qb, kv_norm_g, w_kvb):
    bsz, seq, _ = proj.shape
    inv_freq = ROPE_THETA ** (-jnp.arange(0, MLA_ROPE, 2, dtype=F32) / MLA_ROPE)
    invf = jnp.concatenate([jnp.zeros((PE0,), F32), inv_freq, inv_freq,
                            jnp.zeros((HEAD_PAD - PE0 - MLA_ROPE,), F32)]).reshape(1, LANES)
    wq, wk, wv = _mla_weights(w_qb, w_kvb)
    wide = MLA_HEADS * HEAD_PAD
    row = lambda shape: pl.BlockSpec(shape, lambda b: (0,) * len(shape))
    return pl.pallas_call(
        _mla_kernel,
        grid=(bsz,),
        in_specs=[pl.BlockSpec((1, seq, MLA_Q_RANK), lambda b: (b, 0, COL_MQ // MLA_Q_RANK)),
                  pl.BlockSpec((1, seq, MLA_KV_RANK), lambda b: (b, 0, COL_MKV // MLA_KV_RANK)),
                  pl.BlockSpec((1, seq, LANES), lambda b: (b, 0, COL_MISC // LANES)),
                  pl.BlockSpec((1, seq, 1), lambda b: (b, 0, 0)),
                  row((1, LANES)), row((1, MLA_Q_RANK)), row((1, MLA_KV_RANK)),
                  row((MLA_Q_RANK, wide)), row((MLA_KV_RANK, wide)), row((MLA_KV_RANK, wide))],
        out_specs=pl.BlockSpec((1, seq, MLA_HEADS * MLA_V), lambda b: (b, 0, 0)),
        out_shape=jax.ShapeDtypeStruct((bsz, seq, MLA_HEADS * MLA_V), BF16),
        scratch_shapes=[pltpu.VMEM((seq, wide), BF16), pltpu.VMEM((seq, wide), BF16),
                        pltpu.VMEM((seq, wide), BF16)],
        compiler_params=_cparams(1),
        name="mla_mixer",
    )(proj, proj, proj, positions.reshape(bsz, seq, 1), invf,
      q_norm_g.astype(F32).reshape(1, MLA_Q_RANK), kv_norm_g.astype(F32).reshape(1, MLA_KV_RANK), wq, wk, wv)


FF_TILE = 1024


def _outmlp_kernel(h_ref, oa_ref, ob_ref, oc_ref, gtm_ref, shf_ref, scf_ref, gtf_ref, g_ref, fg_ref,
                   wo_ref, w1_ref, w2_ref, o_ref, *, final_norm):
    na, nb = oa_ref.shape[2], ob_ref.shape[2]
    mix = (_dot(oa_ref[0], wo_ref[0:na, :]) + _dot(ob_ref[0], wo_ref[na:na + nb, :])
           + _dot(oc_ref[0], wo_ref[na + nb:, :]))
    h1 = h_ref[0] + gtm_ref[0] * mix
    u = _modnorm(h1, g_ref[...], scf_ref[0], shf_ref[0]).astype(BF16)
    d_ff = w1_ref.shape[1]
    f = jnp.zeros_like(h1)
    for t in range(d_ff // FF_TILE):
        hid = jnp.maximum(_dot(u, w1_ref[:, t * FF_TILE:(t + 1) * FF_TILE]), 0.0)
        f = f + _dot((hid * hid).astype(BF16), w2_ref[t * FF_TILE:(t + 1) * FF_TILE, :])
    h2 = h1 + gtf_ref[0] * f
    if final_norm:
        ms = jnp.mean(h2 * h2, axis=-1, keepdims=True)
        h2 = h2 * lax.rsqrt(ms + EPS) * fg_ref[...]
    o_ref[0] = h2


def _out_mlp(h, o_a, o_b, o_c, mod_l, g, final_g, w_out, w1, w2, final_norm, tm=512):
    bsz, seq, d = h.shape
    d_ff = w1.shape[1]
    mod3 = mod_l.reshape(bsz, 1, N_MOD * d)
    modspec = lambda m: pl.BlockSpec((1, 1, d), lambda b, i: (b, 0, m))
    tile = lambda n: pl.BlockSpec((1, tm, n), lambda b, i: (b, i, 0))
    const = lambda shape: pl.BlockSpec(shape, lambda b, i: (0,) * len(shape), pipeline_mode=pl.Buffered(1))
    return pl.pallas_call(
        functools.partial(_outmlp_kernel, final_norm=final_norm),
        grid=(bsz, seq // tm),
        in_specs=[tile(d), tile(o_a.shape[2]), tile(o_b.shape[2]), tile(o_c.shape[2]),
                  modspec(2), modspec(3), modspec(4), modspec(5),
                  const((1, d)), const((1, d)),
                  const((d, d)), const((d, d_ff)), const((d_ff, d))],
        out_specs=tile(d),
        out_shape=jax.ShapeDtypeStruct((bsz, seq, d), F32),
        compiler_params=_cparams(2),
        name="outproj_mlp",
    )(h, o_a, o_b, o_c, mod3, mod3, mod3, mod3, g.reshape(1, d), final_g.reshape(1, d), w_out, w1, w2)


def _pad_w_in(w_in_l):
    d = w_in_l.shape[0]
    o = 0
    parts = {}
    for name, n in (("gqkvz", 4 * GDN_W), ("ga", GDN_HEADS), ("gb", GDN_HEADS), ("rx", RG_WIDTH),
                    ("rgate", RG_WIDTH), ("mq", MLA_Q_RANK), ("mkv", MLA_KV_RANK), ("mkr", MLA_ROPE)):
        parts[name] = w_in_l[:, o:o + n]
        o += n
    pad = jnp.zeros((d, NP_IN - COL_MISC - MLA_ROPE - 2 * GDN_HEADS), w_in_l.dtype)
    return jnp.concatenate([parts["gqkvz"], parts["rx"], parts["rgate"], parts["mq"], parts["mkv"],
                            parts["mkr"], parts["ga"], parts["gb"], pad], axis=1).astype(BF16)


def kernel(x, c, positions, w_mod, b_mod, norm_mix_g, w_in, gdn_conv_w, gdn_a_log, gdn_dt_bias, gdn_norm_g, rg_conv_w, rg_conv_b, rg_w_a, rg_b_a, rg_w_x, rg_b_x, rg_lambda, mla_q_norm_g, mla_w_qb, mla_kv_norm_g, mla_w_kvb, w_out, norm_mlp_g, w_mlp_in, w_mlp_out, final_norm_g):
    depth = w_mod.shape[0]
    mod = _modulation(c, w_mod, b_mod)
    h = x
    for l in range(depth):
        proj = _in_projection(h, mod[l], norm_mix_g[l], _pad_w_in(w_in[l]))
        o_a = _gdn_mixer(proj, gdn_conv_w[l], gdn_a_log[l], gdn_dt_bias[l], gdn_norm_g[l])
        o_b = _rg_mixer(proj, rg_conv_w[l], rg_conv_b[l], rg_w_a[l], rg_b_a[l], rg_w_x[l], rg_b_x[l],
                        rg_lambda[l])
        o_c = _mla_mixer(proj, positions, mla_q_norm_g[l], mla_w_qb[l], mla_kv_norm_g[l], mla_w_kvb[l])
        h = _out_mlp(h, o_a, o_b, o_c, mod[l], norm_mlp_g[l], final_norm_g,
                     w_out[l].astype(BF16), w_mlp_in[l].astype(BF16), w_mlp_out[l].astype(BF16),
                     final_norm=(l == depth - 1))
    return h
```

```python
import functools

import jax
import jax.numpy as jnp
from jax import lax
from jax.experimental import pallas as pl
from jax.experimental.pallas import tpu as pltpu

F32 = jnp.float32
BF16 = jnp.bfloat16

EPS = 1e-6
N_MOD = 6
CHUNK = 64
GDN_HEADS = 4
GDN_DK = 64
GDN_W = GDN_HEADS * GDN_DK
GDN_CONV = 4
RG_WIDTH = 512
RG_BLOCK = 64
RG_CONV = 4
RG_C = 8.0
MLA_HEADS = 4
MLA_NOPE = 64
MLA_ROPE = 32
MLA_V = 64
MLA_Q_RANK = 256
MLA_KV_RANK = 128
ROPE_THETA = 10000.0
LANES = 128
SUBLANES = 8
HEAD_PAD = LANES

COL_GDN = 0
COL_RX = 1024
COL_RGATE = 1536
COL_MQ = 2048
COL_MKV = 2304
COL_MISC = 2432
NP_IN = 2560
MISC_A = MLA_ROPE
MISC_B = MLA_ROPE + GDN_HEADS

SLAB = 256
VMEM_LIMIT = 56 * 1024 * 1024


def _cparams(n_axes):
    return pltpu.CompilerParams(dimension_semantics=("arbitrary",) * n_axes,
                                vmem_limit_bytes=VMEM_LIMIT)


def _dot(a, b):
    return jnp.dot(a, b, preferred_element_type=F32)


def _dot_nt(a, b):
    return lax.dot_general(a, b, (((1,), (1,)), ((), ())), preferred_element_type=F32)


def _split3(x):
    hi = x.astype(BF16)
    r1 = x - hi.astype(F32)
    mid = r1.astype(BF16)
    lo = (r1 - mid.astype(F32)).astype(BF16)
    return hi, mid, lo


def _dot3_lhs(x, w):
    hi, mid, lo = _split3(x)
    return _dot(hi, w) + _dot(mid, w) + _dot(lo, w)


def _dot3_rhs(w, x):
    hi, mid, lo = _split3(x)
    return _dot(w, hi) + _dot(w, mid) + _dot(w, lo)


def _sigmoid(x):
    return 1.0 / (1.0 + jnp.exp(-x))


def _silu(x):
    return x * _sigmoid(x)


def _softplus(x):
    return jnp.maximum(x, 0.0) + jnp.log1p(jnp.exp(-jnp.abs(x)))


def _causal_conv(ref, r0, rows, col0, ncols, w):
    taps = w.shape[0]
    cur = ref[0, pl.ds(r0, rows), col0:col0 + ncols]
    pstart = pl.multiple_of(jnp.maximum(r0 - SUBLANES, 0), SUBLANES)
    prev = ref[0, pl.ds(pstart, SUBLANES), col0:col0 + ncols]
    prev = jnp.where(r0 > 0, prev, 0.0)
    xcat = jnp.concatenate([prev, cur], axis=0)
    acc = cur * w[taps - 1:taps, :]
    for j in range(taps - 1):
        shifted = pltpu.roll(xcat, taps - 1 - j, axis=0)[SUBLANES:, :]
        acc = acc + shifted * w[j:j + 1, :]
    return acc


def _mod_kernel(c_ref, w_ref, b_ref, o_ref):
    ca = _silu(c_ref[...])
    o_ref[0] = _dot(ca.astype(BF16), w_ref[0].astype(BF16)) + b_ref[0]


def _modulation(c, w_mod, b_mod):
    depth, d, n = w_mod.shape
    bsz = c.shape[0]
    tn = 1536
    return pl.pallas_call(
        _mod_kernel,
        grid=(depth, n // tn),
        in_specs=[pl.BlockSpec((bsz, d), lambda l, j: (0, 0)),
                  pl.BlockSpec((1, d, tn), lambda l, j: (l, 0, j)),
                  pl.BlockSpec((1, 1, tn), lambda l, j: (l, 0, j))],
        out_specs=pl.BlockSpec((1, bsz, tn), lambda l, j: (l, 0, j)),
        out_shape=jax.ShapeDtypeStruct((depth, bsz, n), F32),
        compiler_params=_cparams(2),
        name="adaln_mod",
    )(c, w_mod, b_mod.reshape(depth, 1, n))


def _modnorm(x, g, sc, sh):
    ms = jnp.mean(x * x, axis=-1, keepdims=True)
    return (x * lax.rsqrt(ms + EPS) * g) * (1.0 + sc) + sh


def _inproj_kernel(h_ref, sh_ref, sc_ref, g_ref, w_ref, o_ref):
    u = _modnorm(h_ref[0], g_ref[...], sc_ref[0], sh_ref[0])
    o_ref[0] = _dot(u.astype(BF16), w_ref[...])


def _in_projection(h, mod_l, g, w_pad, tm=512):
    bsz, seq, d = h.shape
    npad = w_pad.shape[1]
    mod3 = mod_l.reshape(bsz, 1, N_MOD * d)
    return pl.pallas_call(
        _inproj_kernel,
        grid=(bsz, seq // tm),
        in_specs=[pl.BlockSpec((1, tm, d), lambda b, i: (b, i, 0)),
                  pl.BlockSpec((1, 1, d), lambda b, i: (b, 0, 0)),
                  pl.BlockSpec((1, 1, d), lambda b, i: (b, 0, 1)),
                  pl.BlockSpec((1, d), lambda b, i: (0, 0)),
                  pl.BlockSpec((d, npad), lambda b, i: (0, 0))],
        out_specs=pl.BlockSpec((1, tm, npad), lambda b, i: (b, i, 0)),
        out_shape=jax.ShapeDtypeStruct((bsz, seq, npad), F32),
        compiler_params=_cparams(2),
        name="norm_inproj",
    )(h, mod3, mod3, g.reshape(1, d), w_pad)


def _head_of(idx):
    return idx // GDN_DK


def _blockdiag(x):
    xb = x.astype(BF16)
    lane_head = _head_of(lax.broadcasted_iota(jnp.int32, xb.shape, 1))
    zero = jnp.zeros_like(xb)
    return jnp.concatenate([jnp.where(lane_head == hh, xb, zero) for hh in range(GDN_HEADS)], axis=0)


def _gdn_kernel(qkvz_ref, misc_ref, convw_ref, nalog_ref, dtb_ref, ng_ref, o_ref,
                u_s, w_s, qd_s, qk_s, ktt_s, cd_s, o_s, st_s):
    seq = qkvz_ref.shape[1]
    n_slab = seq // SLAB
    n_chunk = seq // CHUNK
    per_slab = SLAB // CHUNK

    ii = lax.broadcasted_iota(jnp.int32, (GDN_W, GDN_W), 0)
    jj = lax.broadcasted_iota(jnp.int32, (GDN_W, GDN_W), 1)
    same_blk = _head_of(ii) == _head_of(jj)
    ones_bd = jnp.where(same_blk, 1.0, 0.0).astype(BF16)
    tri_bd = jnp.where(same_blk & (ii >= jj), 1.0, 0.0).astype(BF16)
    mi = lax.broadcasted_iota(jnp.int32, (LANES, GDN_W), 0)
    mj = lax.broadcasted_iota(jnp.int32, (LANES, GDN_W), 1)
    sel_a = jnp.where(mi == MISC_A + _head_of(mj), 1.0, 0.0).astype(BF16)
    sel_b = jnp.where(mi == MISC_B + _head_of(mj), 1.0, 0.0).astype(BF16)
    ci = lax.broadcasted_iota(jnp.int32, (CHUNK, GDN_W), 0)
    cj = lax.broadcasted_iota(jnp.int32, (CHUNK, GDN_W), 1)
    cs = cj % CHUNK
    lower_incl = ci >= cs
    strict_lower = ci > cs
    eye_hl = jnp.where(ci == cs, 1.0, 0.0)
    first_lane = cs == 0
    ones_lhs = jnp.ones((CHUNK, GDN_W), BF16)
    convw = convw_ref[...]

    def slab_body(si, carry):
        r0 = pl.multiple_of(si * SLAB, SLAB)
        y = _silu(_causal_conv(qkvz_ref, r0, SLAB, 0, 3 * GDN_W, convw))
        q, k, v = y[:, :GDN_W], y[:, GDN_W:2 * GDN_W], y[:, 2 * GDN_W:]

        def l2n(x):
            xx = x * x
            hi = xx.astype(BF16)
            lo = (xx - hi.astype(F32)).astype(BF16)
            ss = _dot(hi, ones_bd) + _dot(lo, ones_bd)
            return x * lax.rsqrt(ss + EPS)

        q = l2n(q) * (GDN_DK ** -0.5)
        k = l2n(k)
        misc = misc_ref[0, pl.ds(r0, SLAB), :]
        a_exp = _dot3_lhs(misc, sel_a)
        b_exp = _dot3_lhs(misc, sel_b)
        g = nalog_ref[...] * _softplus(a_exp + dtb_ref[...])
        beta = _sigmoid(b_exp)
        gcum = _dot3_rhs(tri_bd, g)
        glast = _dot3_rhs(ones_bd, g)
        eg = jnp.exp(gcum)
        kb = k * beta
        vb = v * beta
        kbg = kb * eg
        qd_s[pl.ds(r0, SLAB), :] = (q * eg).astype(BF16)
        kt = k * jnp.exp(glast - gcum)
        cdec = jnp.exp(glast)

        rng = [(c * CHUNK, (c + 1) * CHUNK) for c in range(per_slab)]
        rows = [pl.multiple_of(r0 + a, CHUNK) for a, _ in rng]
        aqs = [_dot_nt(jnp.concatenate([kb[a:b], q[a:b]], axis=0).astype(BF16), _blockdiag(k[a:b]))
               for a, b in rng]

        def row_of_cumsum(gc):
            gr = jnp.zeros((CHUNK, GDN_W), F32)
            for piece in _split3(jnp.where(first_lane, gc, 0.0)):
                gr = gr + _dot_nt(ones_lhs, _blockdiag(piece))
            return gr

        decays = [jnp.exp(jnp.where(lower_incl, gcum[a:b] - row_of_cumsum(gcum[a:b]), 0.0)) for a, b in rng]
        lmats = [jnp.where(strict_lower, aq[:CHUNK] * dec, 0.0) for aq, dec in zip(aqs, decays)]
        for row0, aq, dec in zip(rows, aqs, decays):
            qk_s[pl.ds(row0, CHUNK), :] = jnp.where(lower_incl, aq[CHUNK:] * dec, 0.0).astype(BF16)
        dinvs = [eye_hl - jnp.where(ci // 2 == cs // 2, lm, 0.0) for lm in lmats]
        blk = 2
        while blk < CHUNK:
            off_diag = (ci // (2 * blk) == cs // (2 * blk)) & (ci // blk != cs // blk)
            ps = [_dot(d.astype(BF16), _blockdiag(jnp.where(off_diag, lm, 0.0))) for d, lm in zip(dinvs, lmats)]
            dinvs = [d - _dot(p.astype(BF16), _blockdiag(d)) for d, p in zip(dinvs, ps)]
            blk *= 2
        for c, ((a, b), row0, d) in enumerate(zip(rng, rows, dinvs)):
            tb = d.astype(BF16)
            u_s[pl.ds(row0, CHUNK), :] = _dot(tb, _blockdiag(vb[a:b]))
            w_s[pl.ds(row0, CHUNK), :] = _dot(tb, _blockdiag(kbg[a:b])).astype(BF16)
            ktc = jnp.concatenate([kt[a:b], jnp.zeros((LANES - CHUNK, GDN_W), F32)], axis=0)
            ktt_s[si * per_slab + c] = ktc.T.astype(BF16)
            cd_s[si * per_slab + c] = cdec[a:a + SUBLANES]
        return carry

    lax.fori_loop(0, n_slab, slab_body, 0)

    st_s[...] = jnp.zeros_like(st_s)

    def chunk_body(ci_idx, carry):
        row0 = pl.multiple_of(ci_idx * CHUNK, CHUNK)
        sb = st_s[...].astype(BF16)
        lhs = jnp.concatenate([w_s[pl.ds(row0, CHUNK), :], qd_s[pl.ds(row0, CHUNK), :]], axis=0)
        wq = _dot(lhs, sb)
        v_new = u_s[pl.ds(row0, CHUNK), :] - wq[:CHUNK]
        o_s[pl.ds(row0, CHUNK), :] = wq[CHUNK:] + _dot(qk_s[pl.ds(row0, CHUNK), :], _blockdiag(v_new))
        vpad = jnp.concatenate([v_new.astype(BF16), jnp.zeros((LANES - CHUNK, GDN_W), BF16)], axis=0)
        upd = _dot(ktt_s[ci_idx], vpad)
        st_s[...] = st_s[...] * cd_s[ci_idx][0:1, :] + jnp.where(same_blk, upd, 0.0)
        return carry

    lax.fori_loop(0, n_chunk, chunk_body, 0)

    mean_bd = jnp.where(same_blk, 1.0 / GDN_DK, 0.0).astype(BF16)

    def out_body(si, carry):
        r0 = pl.multiple_of(si * SLAB, SLAB)
        o = o_s[pl.ds(r0, SLAB), :]
        oo = o * o
        hi = oo.astype(BF16)
        lo = (oo - hi.astype(F32)).astype(BF16)
        ms = _dot(hi, mean_bd) + _dot(lo, mean_bd)
        y = o * lax.rsqrt(ms + EPS) * ng_ref[...]
        z = qkvz_ref[0, pl.ds(r0, SLAB), 3 * GDN_W:4 * GDN_W]
        o_ref[0, pl.ds(r0, SLAB), :] = (y * _silu(z)).astype(o_ref.dtype)
        return carry

    lax.fori_loop(0, n_slab, out_body, 0)


def _gdn_mixer(proj, conv_w, a_log, dt_bias, norm_g):
    bsz, seq, _ = proj.shape
    n_chunk = seq // CHUNK
    nalog = jnp.repeat(-jnp.exp(a_log.astype(F32)), GDN_DK).reshape(1, GDN_W)
    dtb = jnp.repeat(dt_bias.astype(F32), GDN_DK).reshape(1, GDN_W)
    ng = jnp.tile(norm_g.astype(F32), GDN_HEADS).reshape(1, GDN_W)
    row = lambda shape: pl.BlockSpec(shape, lambda b: (0,) * len(shape))
    return pl.pallas_call(
        _gdn_kernel,
        grid=(bsz,),
        in_specs=[pl.BlockSpec((1, seq, 4 * GDN_W), lambda b: (b, 0, COL_GDN // (4 * GDN_W))),
                  pl.BlockSpec((1, seq, LANES), lambda b: (b, 0, COL_MISC // LANES)),
                  row((GDN_CONV, 3 * GDN_W)), row((1, GDN_W)), row((1, GDN_W)), row((1, GDN_W))],
        out_specs=pl.BlockSpec((1, seq, GDN_W), lambda b: (b, 0, 0)),
        out_shape=jax.ShapeDtypeStruct((bsz, seq, GDN_W), BF16),
        scratch_shapes=[pltpu.VMEM((seq, GDN_W), F32),
                        pltpu.VMEM((seq, GDN_W), BF16),
                        pltpu.VMEM((seq, GDN_W), BF16),
                        pltpu.VMEM((seq, GDN_W), BF16),
                        pltpu.VMEM((n_chunk, GDN_W, LANES), BF16),
                        pltpu.VMEM((n_chunk, SUBLANES, GDN_W), F32),
                        pltpu.VMEM((seq, GDN_W), F32),
                        pltpu.VMEM((GDN_W, GDN_W), F32)],
        compiler_params=_cparams(1),
        name="gdn_mixer",
    )(proj, proj, conv_w, nalog, dtb, ng)


def _gelu_tanh(x):
    return 0.5 * x * (1.0 + jnp.tanh(0.7978845608028654 * (x + 0.044715 * (x * x * x))))


def _rg_kernel(rx_ref, gate_ref, convw_ref, convb_ref, w_ref, ba_ref, bx_ref, lam_ref, o_ref, hc_s):
    seq = rx_ref.shape[1]
    n_slab = seq // SLAB
    n_tile = SLAB // SUBLANES
    hc_s[...] = jnp.zeros_like(hc_s)
    convw = convw_ref[...]
    sub = lax.broadcasted_iota(jnp.int32, (n_tile, SUBLANES, LANES), 1)

    def slab_body(si, carry):
        r0 = pl.multiple_of(si * SLAB, SLAB)
        xc_all = _causal_conv(rx_ref, r0, SLAB, 0, RG_WIDTH, convw) + convb_ref[...]
        for gi in range(RG_WIDTH // LANES):
            l0, l1 = gi * LANES, (gi + 1) * LANES
            xc = xc_all[:, l0:l1]
            pre = _dot(xc.astype(BF16), w_ref[gi])
            r = _sigmoid(pre[:, :LANES] + ba_ref[:, l0:l1])
            ig = _sigmoid(pre[:, LANES:] + bx_ref[:, l0:l1])
            log_a = (-RG_C) * r * _softplus(-lam_ref[:, l0:l1])
            a = jnp.exp(log_a)
            mult = jnp.sqrt(1.0 - a * a)
            bt = mult * (ig * xc)
            a3 = a.reshape(n_tile, SUBLANES, LANES)
            b3 = bt.reshape(n_tile, SUBLANES, LANES)
            d = 1
            while d < SUBLANES:
                a_sh = pltpu.roll(a3, d, axis=1)
                b_sh = pltpu.roll(b3, d, axis=1)
                keep = sub >= d
                b3 = jnp.where(keep, a3 * b_sh + b3, b3)
                a3 = jnp.where(keep, a3 * a_sh, a3)
                d *= 2
            car = hc_s[:, l0:l1]
            hs = []
            for n in range(n_tile):
                hn = a3[n] * car + b3[n]
                hs.append(hn)
                car = jnp.broadcast_to(hn[SUBLANES - 1:SUBLANES, :], (SUBLANES, LANES))
            hc_s[:, l0:l1] = car
            hh = jnp.concatenate(hs, axis=0)
            gate = gate_ref[0, pl.ds(r0, SLAB), l0:l1]
            o_ref[0, pl.ds(r0, SLAB), l0:l1] = (hh * _gelu_tanh(gate)).astype(o_ref.dtype)
        return carry

    lax.fori_loop(0, n_slab, slab_body, 0)


def _rg_weights(w_a, w_x):
    def pair_bd(w):
        z = jnp.zeros((RG_BLOCK, RG_BLOCK), w.dtype)
        out = []
        for gi in range(RG_WIDTH // LANES):
            top = jnp.concatenate([w[2 * gi], z], axis=1)
            bot = jnp.concatenate([z, w[2 * gi + 1]], axis=1)
            out.append(jnp.concatenate([top, bot], axis=0))
        return jnp.stack(out)
    return jnp.concatenate([pair_bd(w_a), pair_bd(w_x)], axis=2).astype(BF16)


def _rg_mixer(proj, conv_w, conv_b, w_a, b_a, w_x, b_x, lam):
    bsz, seq, _ = proj.shape
    row = lambda shape: pl.BlockSpec(shape, lambda b: (0,) * len(shape))
    r2 = lambda v: v.astype(F32).reshape(1, RG_WIDTH)
    return pl.pallas_call(
        _rg_kernel,
        grid=(bsz,),
        in_specs=[pl.BlockSpec((1, seq, RG_WIDTH), lambda b: (b, 0, COL_RX // RG_WIDTH)),
                  pl.BlockSpec((1, seq, RG_WIDTH), lambda b: (b, 0, COL_RGATE // RG_WIDTH)),
                  row((RG_CONV, RG_WIDTH)), row((1, RG_WIDTH)),
                  row((RG_WIDTH // LANES, LANES, 2 * LANES)),
                  row((1, RG_WIDTH)), row((1, RG_WIDTH)), row((1, RG_WIDTH))],
        out_specs=pl.BlockSpec((1, seq, RG_WIDTH), lambda b: (b, 0, 0)),
        out_shape=jax.ShapeDtypeStruct((bsz, seq, RG_WIDTH), BF16),
        scratch_shapes=[pltpu.VMEM((SUBLANES, RG_WIDTH), F32)],
        compiler_params=_cparams(1),
        name="rglru_mixer",
    )(proj, proj, conv_w, r2(conv_b), _rg_weights(w_a, w_x), r2(b_a), r2(b_x), r2(lam))


MLA_TQ = SLAB
PE0 = MLA_NOPE
PE_HALF = MLA_ROPE // 2


def _rms(x, g):
    ms = jnp.mean(x * x, axis=-1, keepdims=True)
    return x * lax.rsqrt(ms + EPS) * g


def _mla_kernel(mq_ref, mkv_ref, misc_ref, pos_ref, invf_ref, qg_ref, kvg_ref, wq_ref, wk_ref, wvt_ref,
                o_ref, q_s, k_s, vt_s):
    seq = mq_ref.shape[1]
    n_slab = seq // SLAB
    n_qblk = seq // MLA_TQ
    lane = lax.broadcasted_iota(jnp.int32, (SLAB, LANES), 1)
    x1_mask = (lane >= PE0) & (lane < PE0 + PE_HALF)
    x2_mask = (lane >= PE0 + PE_HALF) & (lane < PE0 + MLA_ROPE)
    scale = (MLA_NOPE + MLA_ROPE) ** -0.5

    def prep_body(si, carry):
        r0 = pl.multiple_of(si * SLAB, SLAB)
        ang = pos_ref[0, pl.ds(r0, SLAB), :].astype(F32) * invf_ref[...]
        cos = jnp.cos(ang)
        sin = jnp.sin(ang)
        c_neg = jnp.where(x1_mask, -sin, 0.0)
        c_pos = jnp.where(x2_mask, sin, 0.0)

        def rope(t):
            return t * cos + pltpu.roll(t, LANES - PE_HALF, axis=1) * c_neg + pltpu.roll(t, PE_HALF, axis=1) * c_pos

        qn = _rms(mq_ref[0, pl.ds(r0, SLAB), :], qg_ref[...]).astype(BF16)
        q = _dot(qn, wq_ref[...])
        kvn = _rms(mkv_ref[0, pl.ds(r0, SLAB), :], kvg_ref[...]).astype(BF16)
        kn = _dot(kvn, wk_ref[...])
        vt_s[si] = _dot_nt(wvt_ref[...], kvn).astype(BF16)
        kr = pltpu.roll(misc_ref[0, pl.ds(r0, SLAB), :], PE0, axis=1)
        kr = rope(jnp.where(x1_mask | x2_mask, kr, 0.0))
        for hh in range(MLA_HEADS):
            l0, l1 = hh * HEAD_PAD, (hh + 1) * HEAD_PAD
            q_s[pl.ds(r0, SLAB), l0:l1] = (rope(q[:, l0:l1]) * scale).astype(BF16)
            k_s[pl.ds(r0, SLAB), l0:l1] = (kn[:, l0:l1] + kr).astype(BF16)
        return carry

    lax.fori_loop(0, n_slab, prep_body, 0)

    ki = lax.broadcasted_iota(jnp.int32, (MLA_TQ, MLA_TQ), 0)
    qi_ = lax.broadcasted_iota(jnp.int32, (MLA_TQ, MLA_TQ), 1)
    diag_mask = (ki // CHUNK) <= (qi_ // CHUNK)

    def q_body(qi, carry):
        q0 = pl.multiple_of(qi * MLA_TQ, MLA_TQ)

        def attend(kj, state, masked):
            k0 = pl.multiple_of(kj * MLA_TQ, MLA_TQ)
            scores = [_dot_nt(k_s[pl.ds(k0, MLA_TQ), hh * HEAD_PAD:(hh + 1) * HEAD_PAD],
                              q_s[pl.ds(q0, MLA_TQ), hh * HEAD_PAD:(hh + 1) * HEAD_PAD])
                      for hh in range(MLA_HEADS)]
            stats, probs = [], []
            for hh, s in enumerate(scores):
                m, l, _ = state[hh]
                if masked:
                    s = jnp.where(diag_mask, s, -jnp.inf)
                m_new = jnp.maximum(m, jnp.max(s, axis=0, keepdims=True))
                alpha = jnp.exp(m - m_new)
                p = jnp.exp(s - m_new)
                stats.append((m_new, alpha * l + jnp.sum(p, axis=0, keepdims=True), alpha))
                probs.append(p.astype(BF16))
            return tuple(
                (m_new, l, alpha * state[hh][2] + _dot(vt_s[kj, hh * MLA_V:(hh + 1) * MLA_V, :], probs[hh]))
                for hh, (m_new, l, alpha) in enumerate(stats))

        init = tuple((jnp.full((1, MLA_TQ), -jnp.inf, F32), jnp.zeros((1, MLA_TQ), F32),
                      jnp.zeros((MLA_V, MLA_TQ), F32)) for _ in range(MLA_HEADS))
        state = attend(qi, init, True)
        state = lax.fori_loop(0, qi, lambda kj, st: attend(kj, st, False), state)
        out_t = jnp.concatenate([acc / l for (_, l, acc) in state], axis=0)
        o_ref[0, pl.ds(q0, MLA_TQ), :] = out_t.T.astype(o_ref.dtype)
        return carry

    lax.fori_loop(0, n_qblk, q_body, 0)


def _mla_weights(w_qb, w_kvb):
    qr = w_qb.reshape(MLA_Q_RANK, MLA_HEADS, MLA_NOPE + MLA_ROPE)
    zq = jnp.zeros((MLA_Q_RANK, MLA_HEADS, HEAD_PAD - MLA_NOPE - MLA_ROPE), w_qb.dtype)
    wq = jnp.concatenate([qr, zq], axis=2).reshape(MLA_Q_RANK, MLA_HEADS * HEAD_PAD)
    kvr = w_kvb.reshape(MLA_KV_RANK, MLA_HEADS, MLA_NOPE + MLA_V)
    zk = jnp.zeros((MLA_KV_RANK, MLA_HEADS, HEAD_PAD - MLA_NOPE), w_kvb.dtype)
    wk = jnp.concatenate([kvr[:, :, :MLA_NOPE], zk], axis=2).reshape(MLA_KV_RANK, MLA_HEADS * HEAD_PAD)
    wvt = kvr[:, :, MLA_NOPE:].reshape(MLA_KV_RANK, MLA_HEADS * MLA_V).T
    return wq.astype(BF16), wk.astype(BF16), wvt.astype(BF16)


def _mla_mixer(proj, positions, q_norm_g, w_qb, kv_norm_g, w_kvb):
    bsz, seq, _ = proj.shape
    inv_freq = ROPE_THETA ** (-jnp.arange(0, MLA_ROPE, 2, dtype=F32) / MLA_ROPE)
    invf = jnp.concatenate([jnp.zeros((PE0,), F32), inv_freq, inv_freq,
                            jnp.zeros((HEAD_PAD - PE0 - MLA_ROPE,), F32)]).reshape(1, LANES)
    wq, wk, wvt = _mla_weights(w_qb, w_kvb)
    wide = MLA_HEADS * HEAD_PAD
    vw = MLA_HEADS * MLA_V
    row = lambda shape: pl.BlockSpec(shape, lambda b: (0,) * len(shape))
    return pl.pallas_call(
        _mla_kernel,
        grid=(bsz,),
        in_specs=[pl.BlockSpec((1, seq, MLA_Q_RANK), lambda b: (b, 0, COL_MQ // MLA_Q_RANK)),
                  pl.BlockSpec((1, seq, MLA_KV_RANK), lambda b: (b, 0, COL_MKV // MLA_KV_RANK)),
                  pl.BlockSpec((1, seq, LANES), lambda b: (b, 0, COL_MISC // LANES)),
                  pl.BlockSpec((1, seq, 1), lambda b: (b, 0, 0)),
                  row((1, LANES)), row((1, MLA_Q_RANK)), row((1, MLA_KV_RANK)),
                  row((MLA_Q_RANK, wide)), row((MLA_KV_RANK, wide)), row((vw, MLA_KV_RANK))],
        out_specs=pl.BlockSpec((1, seq, vw), lambda b: (b, 0, 0)),
        out_shape=jax.ShapeDtypeStruct((bsz, seq, vw), BF16),
        scratch_shapes=[pltpu.VMEM((seq, wide), BF16), pltpu.VMEM((seq, wide), BF16),
                        pltpu.VMEM((seq // SLAB, vw, SLAB), BF16)],
        compiler_params=_cparams(1),
        name="mla_mixer",
    )(proj, proj, proj, positions.reshape(bsz, seq, 1), invf,
      q_norm_g.astype(F32).reshape(1, MLA_Q_RANK), kv_norm_g.astype(F32).reshape(1, MLA_KV_RANK), wq, wk, wvt)


FF_TILE = 1024


def _outmlp_kernel(h_ref, oa_ref, ob_ref, oc_ref, gtm_ref, shf_ref, scf_ref, gtf_ref, g_ref, fg_ref,
                   wo_ref, w1_ref, w2_ref, o_ref, *, final_norm):
    na, nb = oa_ref.shape[2], ob_ref.shape[2]
    mix = (_dot(oa_ref[0], wo_ref[0:na, :]) + _dot(ob_ref[0], wo_ref[na:na + nb, :])
           + _dot(oc_ref[0], wo_ref[na + nb:, :]))
    h1 = h_ref[0] + gtm_ref[0] * mix
    u = _modnorm(h1, g_ref[...], scf_ref[0], shf_ref[0]).astype(BF16)
    d_ff = w1_ref.shape[1]
    f = jnp.zeros_like(h1)
    for t in range(d_ff // FF_TILE):
        hid = jnp.maximum(_dot(u, w1_ref[:, t * FF_TILE:(t + 1) * FF_TILE]), 0.0)
        f = f + _dot((hid * hid).astype(BF16), w2_ref[t * FF_TILE:(t + 1) * FF_TILE, :])
    h2 = h1 + gtf_ref[0] * f
    if final_norm:
        ms = jnp.mean(h2 * h2, axis=-1, keepdims=True)
        h2 = h2 * lax.rsqrt(ms + EPS) * fg_ref[...]
    o_ref[0] = h2


def _out_mlp(h, o_a, o_b, o_c, mod_l, g, final_g, w_out, w1, w2, final_norm, tm=512):
    bsz, seq, d = h.shape
    d_ff = w1.shape[1]
    mod3 = mod_l.reshape(bsz, 1, N_MOD * d)
    modspec = lambda m: pl.BlockSpec((1, 1, d), lambda b, i: (b, 0, m))
    tile = lambda n: pl.BlockSpec((1, tm, n), lambda b, i: (b, i, 0))
    const = lambda shape: pl.BlockSpec(shape, lambda b, i: (0,) * len(shape), pipeline_mode=pl.Buffered(1))
    return pl.pallas_call(
        functools.partial(_outmlp_kernel, final_norm=final_norm),
        grid=(bsz, seq // tm),
        in_specs=[tile(d), tile(o_a.shape[2]), tile(o_b.shape[2]), tile(o_c.shape[2]),
                  modspec(2), modspec(3), modspec(4), modspec(5),
                  const((1, d)), const((1, d)),
                  const((d, d)), const((d, d_ff)), const((d_ff, d))],
        out_specs=tile(d),
        out_shape=jax.ShapeDtypeStruct((bsz, seq, d), F32),
        compiler_params=_cparams(2),
        name="outproj_mlp",
    )(h, o_a, o_b, o_c, mod3, mod3, mod3, mod3, g.reshape(1, d), final_g.reshape(1, d), w_out, w1, w2)


def _pad_w_in(w_in_l):
    d = w_in_l.shape[0]
    o = 0
    parts = {}
    for name, n in (("gqkvz", 4 * GDN_W), ("ga", GDN_HEADS), ("gb", GDN_HEADS), ("rx", RG_WIDTH),
                    ("rgate", RG_WIDTH), ("mq", MLA_Q_RANK), ("mkv", MLA_KV_RANK), ("mkr", MLA_ROPE)):
        parts[name] = w_in_l[:, o:o + n]
        o += n
    pad = jnp.zeros((d, NP_IN - COL_MISC - MLA_ROPE - 2 * GDN_HEADS), w_in_l.dtype)
    return jnp.concatenate([parts["gqkvz"], parts["rx"], parts["rgate"], parts["mq"], parts["mkv"],
                            parts["mkr"], parts["ga"], parts["gb"], pad], axis=1).astype(BF16)


def kernel(x, c, positions, w_mod, b_mod, norm_mix_g, w_in, gdn_conv_w, gdn_a_log, gdn_dt_bias, gdn_norm_g, rg_conv_w, rg_conv_b, rg_w_a, rg_b_a, rg_w_x, rg_b_x, rg_lambda, mla_q_norm_g, mla_w_qb, mla_kv_norm_g, mla_w_kvb, w_out, norm_mlp_g, w_mlp_in, w_mlp_out, final_norm_g):
    depth = w_mod.shape[0]
    mod = _modulation(c, w_mod, b_mod)
    h = x
    for l in range(depth):
        proj = _in_projection(h, mod[l], norm_mix_g[l], _pad_w_in(w_in[l]))
        o_a = _gdn_mixer(proj, gdn_conv_w[l], gdn_a_log[l], gdn_dt_bias[l], gdn_norm_g[l])
        o_b = _rg_mixer(proj, rg_conv_w[l], rg_conv_b[l], rg_w_a[l], rg_b_a[l], rg_w_x[l], rg_b_x[l],
                        rg_lambda[l])
        o_c = _mla_mixer(proj, positions, mla_q_norm_g[l], mla_w_qb[l], mla_kv_norm_g[l], mla_w_kvb[l])
        h = _out_mlp(h, o_a, o_b, o_c, mod[l], norm_mlp_g[l], final_norm_g,
                     w_out[l].astype(BF16), w_mlp_in[l].astype(BF16), w_mlp_out[l].astype(BF16),
                     final_norm=(l == depth - 1))
    return h
```

```python
import functools

import jax
import jax.numpy as jnp
from jax import lax
from jax.experimental import pallas as pl
from jax.experimental.pallas import tpu as pltpu

F32 = jnp.float32
BF16 = jnp.bfloat16

EPS = 1e-6
N_MOD = 6
CHUNK = 64
GDN_HEADS = 4
GDN_DK = 64
GDN_W = GDN_HEADS * GDN_DK
GDN_CONV = 4
RG_WIDTH = 512
RG_BLOCK = 64
RG_CONV = 4
RG_C = 8.0
MLA_HEADS = 4
MLA_NOPE = 64
MLA_ROPE = 32
MLA_V = 64
MLA_Q_RANK = 256
MLA_KV_RANK = 128
ROPE_THETA = 10000.0
LANES = 128
SUBLANES = 8
HEAD_PAD = LANES

COL_GDN = 0
COL_RX = 1024
COL_RGATE = 1536
COL_MQ = 2048
COL_MKV = 2304
COL_MISC = 2432
NP_IN = 2560
MISC_A = MLA_ROPE
MISC_B = MLA_ROPE + GDN_HEADS

SLAB = 256
VMEM_LIMIT = 56 * 1024 * 1024


def _cparams(n_axes):
    return pltpu.CompilerParams(dimension_semantics=("arbitrary",) * n_axes,
                                vmem_limit_bytes=VMEM_LIMIT)


def _dot(a, b):
    return jnp.dot(a, b, preferred_element_type=F32)


def _dot_nt(a, b):
    return lax.dot_general(a, b, (((1,), (1,)), ((), ())), preferred_element_type=F32)


def _split3(x):
    hi = x.astype(BF16)
    r1 = x - hi.astype(F32)
    mid = r1.astype(BF16)
    lo = (r1 - mid.astype(F32)).astype(BF16)
    return hi, mid, lo


def _dot3_lhs(x, w):
    hi, mid, lo = _split3(x)
    return _dot(hi, w) + _dot(mid, w) + _dot(lo, w)


def _dot3_rhs(w, x):
    hi, mid, lo = _split3(x)
    return _dot(w, hi) + _dot(w, mid) + _dot(w, lo)


def _sigmoid(x):
    return 1.0 / (1.0 + jnp.exp(-x))


def _silu(x):
    return x * _sigmoid(x)


def _softplus(x):
    return jnp.maximum(x, 0.0) + jnp.log1p(jnp.exp(-jnp.abs(x)))


def _causal_conv(ref, r0, rows, col0, ncols, w):
    taps = w.shape[0]
    cur = ref[0, pl.ds(r0, rows), col0:col0 + ncols]
    pstart = pl.multiple_of(jnp.maximum(r0 - SUBLANES, 0), SUBLANES)
    prev = ref[0, pl.ds(pstart, SUBLANES), col0:col0 + ncols]
    prev = jnp.where(r0 > 0, prev, 0.0)
    xcat = jnp.concatenate([prev, cur], axis=0)
    acc = cur * w[taps - 1:taps, :]
    for j in range(taps - 1):
        shifted = pltpu.roll(xcat, taps - 1 - j, axis=0)[SUBLANES:, :]
        acc = acc + shifted * w[j:j + 1, :]
    return acc


def _mod_kernel(c_ref, w_ref, b_ref, o_ref):
    ca = _silu(c_ref[...])
    o_ref[0] = _dot(ca.astype(BF16), w_ref[0].astype(BF16)) + b_ref[0]


def _modulation(c, w_mod, b_mod):
    depth, d, n = w_mod.shape
    bsz = c.shape[0]
    tn = 1536
    return pl.pallas_call(
        _mod_kernel,
        grid=(depth, n // tn),
        in_specs=[pl.BlockSpec((bsz, d), lambda l, j: (0, 0)),
                  pl.BlockSpec((1, d, tn), lambda l, j: (l, 0, j)),
                  pl.BlockSpec((1, 1, tn), lambda l, j: (l, 0, j))],
        out_specs=pl.BlockSpec((1, bsz, tn), lambda l, j: (l, 0, j)),
        out_shape=jax.ShapeDtypeStruct((depth, bsz, n), F32),
        compiler_params=_cparams(2),
        name="adaln_mod",
    )(c, w_mod, b_mod.reshape(depth, 1, n))


def _modnorm(x, g, sc, sh):
    ms = jnp.mean(x * x, axis=-1, keepdims=True)
    return (x * lax.rsqrt(ms + EPS) * g) * (1.0 + sc) + sh


def _inproj_kernel(h_ref, sh_ref, sc_ref, g_ref, w_ref, o_ref):
    u = _modnorm(h_ref[0], g_ref[...], sc_ref[0], sh_ref[0])
    o_ref[0] = _dot(u.astype(BF16), w_ref[...])


def _in_projection(h, mod_l, g, w_pad, tm=512):
    bsz, seq, d = h.shape
    npad = w_pad.shape[1]
    mod3 = mod_l.reshape(bsz, 1, N_MOD * d)
    return pl.pallas_call(
        _inproj_kernel,
        grid=(bsz, seq // tm),
        in_specs=[pl.BlockSpec((1, tm, d), lambda b, i: (b, i, 0)),
                  pl.BlockSpec((1, 1, d), lambda b, i: (b, 0, 0)),
                  pl.BlockSpec((1, 1, d), lambda b, i: (b, 0, 1)),
                  pl.BlockSpec((1, d), lambda b, i: (0, 0)),
                  pl.BlockSpec((d, npad), lambda b, i: (0, 0))],
        out_specs=pl.BlockSpec((1, tm, npad), lambda b, i: (b, i, 0)),
        out_shape=jax.ShapeDtypeStruct((bsz, seq, npad), F32),
        compiler_params=_cparams(2),
        name="norm_inproj",
    )(h, mod3, mod3, g.reshape(1, d), w_pad)


def _head_of(idx):
    return idx // GDN_DK


def _blockdiag(x):
    xb = x.astype(BF16)
    lane_head = _head_of(lax.broadcasted_iota(jnp.int32, xb.shape, 1))
    zero = jnp.zeros_like(xb)
    return jnp.concatenate([jnp.where(lane_head == hh, xb, zero) for hh in range(GDN_HEADS)], axis=0)


def _weave(main, side):
    result = None
    main_live = side_live = True
    while main_live or side_live:
        if side_live:
            try:
                next(side)
            except StopIteration:
                side_live = False
        if main_live:
            try:
                next(main)
            except StopIteration as stop:
                main_live, result = False, stop.value
    return result


def _gdn_kernel(qkvz_ref, misc_ref, convw_ref, nalog_ref, dtb_ref, ng_ref, o_ref, o_s, st_s):
    seq = qkvz_ref.shape[1]
    n_slab = seq // SLAB
    per_slab = SLAB // CHUNK

    ii = lax.broadcasted_iota(jnp.int32, (GDN_W, GDN_W), 0)
    jj = lax.broadcasted_iota(jnp.int32, (GDN_W, GDN_W), 1)
    same_blk = _head_of(ii) == _head_of(jj)
    ones_bd = jnp.where(same_blk, 1.0, 0.0).astype(BF16)
    tri_bd = jnp.where(same_blk & (ii >= jj), 1.0, 0.0).astype(BF16)
    mi = lax.broadcasted_iota(jnp.int32, (LANES, GDN_W), 0)
    mj = lax.broadcasted_iota(jnp.int32, (LANES, GDN_W), 1)
    sel_a = jnp.where(mi == MISC_A + _head_of(mj), 1.0, 0.0).astype(BF16)
    sel_b = jnp.where(mi == MISC_B + _head_of(mj), 1.0, 0.0).astype(BF16)
    ci = lax.broadcasted_iota(jnp.int32, (CHUNK, GDN_W), 0)
    cj = lax.broadcasted_iota(jnp.int32, (CHUNK, GDN_W), 1)
    cs = cj % CHUNK
    lower_incl = ci >= cs
    strict_lower = ci > cs
    eye_hl = jnp.where(ci == cs, 1.0, 0.0)
    first_lane = cs == 0
    ones_lhs = jnp.ones((CHUNK, GDN_W), BF16)
    convw = convw_ref[...]
    rng = [(c * CHUNK, (c + 1) * CHUNK) for c in range(per_slab)]

    def prepare(si):
        r0 = pl.multiple_of(si * SLAB, SLAB)
        y = _silu(_causal_conv(qkvz_ref, r0, SLAB, 0, 3 * GDN_W, convw))
        q, k, v = y[:, :GDN_W], y[:, GDN_W:2 * GDN_W], y[:, 2 * GDN_W:]

        def l2n(x):
            xx = x * x
            hi = xx.astype(BF16)
            lo = (xx - hi.astype(F32)).astype(BF16)
            ss = _dot(hi, ones_bd) + _dot(lo, ones_bd)
            return x * lax.rsqrt(ss + EPS)

        q = l2n(q) * (GDN_DK ** -0.5)
        k = l2n(k)
        yield
        misc = misc_ref[0, pl.ds(r0, SLAB), :]
        a_exp = _dot3_lhs(misc, sel_a)
        b_exp = _dot3_lhs(misc, sel_b)
        g = nalog_ref[...] * _softplus(a_exp + dtb_ref[...])
        beta = _sigmoid(b_exp)
        yield
        gcum = _dot3_rhs(tri_bd, g)
        glast = _dot3_rhs(ones_bd, g)
        eg = jnp.exp(gcum)
        kb = k * beta
        vb = v * beta
        kbg = kb * eg
        qd = (q * eg).astype(BF16)
        kt = k * jnp.exp(glast - gcum)
        cdec = jnp.exp(glast)
        yield
        aqs = [_dot_nt(jnp.concatenate([kb[a:b], q[a:b]], axis=0).astype(BF16), _blockdiag(k[a:b]))
               for a, b in rng]
        yield

        def row_of_cumsum(gc):
            gr = jnp.zeros((CHUNK, GDN_W), F32)
            for piece in _split3(jnp.where(first_lane, gc, 0.0)):
                gr = gr + _dot_nt(ones_lhs, _blockdiag(piece))
            return gr

        decays = [jnp.exp(jnp.where(lower_incl, gcum[a:b] - row_of_cumsum(gcum[a:b]), 0.0)) for a, b in rng]
        yield
        lmats = [jnp.where(strict_lower, aq[:CHUNK] * dec, 0.0) for aq, dec in zip(aqs, decays)]
        qks = [jnp.where(lower_incl, aq[CHUNK:] * dec, 0.0).astype(BF16) for aq, dec in zip(aqs, decays)]
        dinvs = [eye_hl - jnp.where(ci // 2 == cs // 2, lm, 0.0) for lm in lmats]
        blk = 2
        while blk < CHUNK:
            off_diag = (ci // (2 * blk) == cs // (2 * blk)) & (ci // blk != cs // blk)
            ps = [_dot(d.astype(BF16), _blockdiag(jnp.where(off_diag, lm, 0.0))) for d, lm in zip(dinvs, lmats)]
            yield
            dinvs = [d - _dot(p.astype(BF16), _blockdiag(d)) for d, p in zip(dinvs, ps)]
            yield
            blk *= 2
        ops = []
        for (a, b), d, qk in zip(rng, dinvs, qks):
            tb = d.astype(BF16)
            u = _dot(tb, _blockdiag(vb[a:b]))
            w = _dot(tb, _blockdiag(kbg[a:b])).astype(BF16)
            ktc = jnp.concatenate([kt[a:b], jnp.zeros((LANES - CHUNK, GDN_W), F32)], axis=0)
            ops.append((u, w, qd[a:b], qk, ktc.T.astype(BF16), cdec[a:a + 1]))
        return ops

    def recur(ops, r_base):
        for c, (u, w, qd, qk, ktt, cd) in enumerate(ops):
            wq = _dot(jnp.concatenate([w, qd], axis=0), st_s[...].astype(BF16))
            yield
            v_new = u - wq[:CHUNK]
            o_s[pl.ds(r_base + c * CHUNK, CHUNK), :] = wq[CHUNK:] + _dot(qk, _blockdiag(v_new))
            vpad = jnp.concatenate([v_new.astype(BF16), jnp.zeros((LANES - CHUNK, GDN_W), BF16)], axis=0)
            upd = _dot(ktt, vpad)
            yield
            st_s[...] = st_s[...] * cd + jnp.where(same_blk, upd, 0.0)
            yield

    def slab_body(si, prev_ops):
        r_prev = pl.multiple_of(jnp.maximum(si - 1, 0) * SLAB, SLAB)
        return _weave(prepare(si), recur(prev_ops, r_prev))

    idle = [(jnp.zeros((CHUNK, GDN_W), F32), jnp.zeros((CHUNK, GDN_W), BF16), jnp.zeros((CHUNK, GDN_W), BF16),
             jnp.zeros((CHUNK, GDN_W), BF16), jnp.zeros((GDN_W, LANES), BF16), jnp.ones((1, GDN_W), F32))
            for _ in range(per_slab)]
    st_s[...] = jnp.zeros_like(st_s)
    last_ops = lax.fori_loop(0, n_slab, slab_body, idle)
    for _ in recur(last_ops, (n_slab - 1) * SLAB):
        pass

    mean_bd = jnp.where(same_blk, 1.0 / GDN_DK, 0.0).astype(BF16)

    def out_body(si, carry):
        r0 = pl.multiple_of(si * SLAB, SLAB)
        o = o_s[pl.ds(r0, SLAB), :]
        oo = o * o
        hi = oo.astype(BF16)
        lo = (oo - hi.astype(F32)).astype(BF16)
        ms = _dot(hi, mean_bd) + _dot(lo, mean_bd)
        y = o * lax.rsqrt(ms + EPS) * ng_ref[...]
        z = qkvz_ref[0, pl.ds(r0, SLAB), 3 * GDN_W:4 * GDN_W]
        o_ref[0, pl.ds(r0, SLAB), :] = (y * _silu(z)).astype(o_ref.dtype)
        return carry

    lax.fori_loop(0, n_slab, out_body, 0)


def _gdn_mixer(proj, conv_w, a_log, dt_bias, norm_g):
    bsz, seq, _ = proj.shape
    nalog = jnp.repeat(-jnp.exp(a_log.astype(F32)), GDN_DK).reshape(1, GDN_W)
    dtb = jnp.repeat(dt_bias.astype(F32), GDN_DK).reshape(1, GDN_W)
    ng = jnp.tile(norm_g.astype(F32), GDN_HEADS).reshape(1, GDN_W)
    row = lambda shape: pl.BlockSpec(shape, lambda b: (0,) * len(shape))
    return pl.pallas_call(
        _gdn_kernel,
        grid=(bsz,),
        in_specs=[pl.BlockSpec((1, seq, 4 * GDN_W), lambda b: (b, 0, COL_GDN // (4 * GDN_W))),
                  pl.BlockSpec((1, seq, LANES), lambda b: (b, 0, COL_MISC // LANES)),
                  row((GDN_CONV, 3 * GDN_W)), row((1, GDN_W)), row((1, GDN_W)), row((1, GDN_W))],
        out_specs=pl.BlockSpec((1, seq, GDN_W), lambda b: (b, 0, 0)),
        out_shape=jax.ShapeDtypeStruct((bsz, seq, GDN_W), BF16),
        scratch_shapes=[pltpu.VMEM((seq, GDN_W), F32),
                        pltpu.VMEM((GDN_W, GDN_W), F32)],
        compiler_params=_cparams(1),
        name="gdn_mixer",
    )(proj, proj, conv_w, nalog, dtb, ng)


def _gelu_tanh(x):
    return 0.5 * x * (1.0 + jnp.tanh(0.7978845608028654 * (x + 0.044715 * (x * x * x))))


def _rg_kernel(rx_ref, gate_ref, convw_ref, convb_ref, w_ref, ba_ref, bx_ref, lam_ref, o_ref, hc_s):
    seq = rx_ref.shape[1]
    n_slab = seq // SLAB
    n_tile = SLAB // SUBLANES
    hc_s[...] = jnp.zeros_like(hc_s)
    convw = convw_ref[...]
    sub = lax.broadcasted_iota(jnp.int32, (n_tile, SUBLANES, LANES), 1)

    def slab_body(si, carry):
        r0 = pl.multiple_of(si * SLAB, SLAB)
        xc_all = _causal_conv(rx_ref, r0, SLAB, 0, RG_WIDTH, convw) + convb_ref[...]
        for gi in range(RG_WIDTH // LANES):
            l0, l1 = gi * LANES, (gi + 1) * LANES
            xc = xc_all[:, l0:l1]
            pre = _dot(xc.astype(BF16), w_ref[gi])
            r = _sigmoid(pre[:, :LANES] + ba_ref[:, l0:l1])
            ig = _sigmoid(pre[:, LANES:] + bx_ref[:, l0:l1])
            log_a = (-RG_C) * r * _softplus(-lam_ref[:, l0:l1])
            a = jnp.exp(log_a)
            mult = jnp.sqrt(1.0 - a * a)
            bt = mult * (ig * xc)
            a3 = a.reshape(n_tile, SUBLANES, LANES)
            b3 = bt.reshape(n_tile, SUBLANES, LANES)
            d = 1
            while d < SUBLANES:
                a_sh = pltpu.roll(a3, d, axis=1)
                b_sh = pltpu.roll(b3, d, axis=1)
                keep = sub >= d
                b3 = jnp.where(keep, a3 * b_sh + b3, b3)
                a3 = jnp.where(keep, a3 * a_sh, a3)
                d *= 2
            car = hc_s[:, l0:l1]
            hs = []
            for n in range(n_tile):
                hn = a3[n] * car + b3[n]
                hs.append(hn)
                car = jnp.broadcast_to(hn[SUBLANES - 1:SUBLANES, :], (SUBLANES, LANES))
            hc_s[:, l0:l1] = car
            hh = jnp.concatenate(hs, axis=0)
            gate = gate_ref[0, pl.ds(r0, SLAB), l0:l1]
            o_ref[0, pl.ds(r0, SLAB), l0:l1] = (hh * _gelu_tanh(gate)).astype(o_ref.dtype)
        return carry

    lax.fori_loop(0, n_slab, slab_body, 0)


def _rg_weights(w_a, w_x):
    def pair_bd(w):
        z = jnp.zeros((RG_BLOCK, RG_BLOCK), w.dtype)
        out = []
        for gi in range(RG_WIDTH // LANES):
            top = jnp.concatenate([w[2 * gi], z], axis=1)
            bot = jnp.concatenate([z, w[2 * gi + 1]], axis=1)
            out.append(jnp.concatenate([top, bot], axis=0))
        return jnp.stack(out)
    return jnp.concatenate([pair_bd(w_a), pair_bd(w_x)], axis=2).astype(BF16)


def _rg_mixer(proj, conv_w, conv_b, w_a, b_a, w_x, b_x, lam):
    bsz, seq, _ = proj.shape
    row = lambda shape: pl.BlockSpec(shape, lambda b: (0,) * len(shape))
    r2 = lambda v: v.astype(F32).reshape(1, RG_WIDTH)
    return pl.pallas_call(
        _rg_kernel,
        grid=(bsz,),
        in_specs=[pl.BlockSpec((1, seq, RG_WIDTH), lambda b: (b, 0, COL_RX // RG_WIDTH)),
                  pl.BlockSpec((1, seq, RG_WIDTH), lambda b: (b, 0, COL_RGATE // RG_WIDTH)),
                  row((RG_CONV, RG_WIDTH)), row((1, RG_WIDTH)),
                  row((RG_WIDTH // LANES, LANES, 2 * LANES)),
                  row((1, RG_WIDTH)), row((1, RG_WIDTH)), row((1, RG_WIDTH))],
        out_specs=pl.BlockSpec((1, seq, RG_WIDTH), lambda b: (b, 0, 0)),
        out_shape=jax.ShapeDtypeStruct((bsz, seq, RG_WIDTH), BF16),
        scratch_shapes=[pltpu.VMEM((SUBLANES, RG_WIDTH), F32)],
        compiler_params=_cparams(1),
        name="rglru_mixer",
    )(proj, proj, conv_w, r2(conv_b), _rg_weights(w_a, w_x), r2(b_a), r2(b_x), r2(lam))


MLA_TQ = SLAB
PE0 = MLA_NOPE
PE_HALF = MLA_ROPE // 2


def _rms(x, g):
    ms = jnp.mean(x * x, axis=-1, keepdims=True)
    return x * lax.rsqrt(ms + EPS) * g


def _mla_kernel(mq_ref, mkv_ref, misc_ref, pos_ref, invf_ref, qg_ref, kvg_ref, wqt_ref, wk_ref, wvt_ref,
                o_ref, qt_s, k_s, vt_s):
    seq = mq_ref.shape[1]
    n_slab = seq // SLAB
    n_qblk = seq // MLA_TQ
    scale = (MLA_NOPE + MLA_ROPE) ** -0.5
    x1_0, x2_0, pe_end = PE0, PE0 + PE_HALF, PE0 + MLA_ROPE

    def prep_body(si, carry):
        r0 = pl.multiple_of(si * SLAB, SLAB)
        ang_t = invf_ref[...] * pos_ref[0, si].astype(F32)
        cos_t = jnp.cos(ang_t)
        sin_t = jnp.sin(ang_t)

        def rope_t(x1, x2):
            return x1 * cos_t - x2 * sin_t, x2 * cos_t + x1 * sin_t

        qn = _rms(mq_ref[0, pl.ds(r0, SLAB), :], qg_ref[...]).astype(BF16)
        q_t = _dot_nt(wqt_ref[...], qn)
        for hh in range(MLA_HEADS):
            blk = q_t[hh * HEAD_PAD:(hh + 1) * HEAD_PAD]
            x1, x2 = rope_t(blk[x1_0:x2_0], blk[x2_0:pe_end])
            roped = jnp.concatenate([blk[:PE0], x1, x2, blk[pe_end:]], axis=0)
            qt_s[si, hh * HEAD_PAD:(hh + 1) * HEAD_PAD, :] = (roped * scale).astype(BF16)
        kvn = _rms(mkv_ref[0, pl.ds(r0, SLAB), :], kvg_ref[...]).astype(BF16)
        kn = _dot(kvn, wk_ref[...])
        vt_s[si] = _dot_nt(wvt_ref[...], kvn).astype(BF16)
        misc_t = misc_ref[0, pl.ds(r0, SLAB), :].T
        k1, k2 = rope_t(misc_t[:PE_HALF], misc_t[PE_HALF:MLA_ROPE])
        kr = jnp.concatenate([jnp.zeros((PE0, SLAB), F32), k1, k2,
                              jnp.zeros((HEAD_PAD - pe_end, SLAB), F32)], axis=0).T
        for hh in range(MLA_HEADS):
            l0, l1 = hh * HEAD_PAD, (hh + 1) * HEAD_PAD
            k_s[pl.ds(r0, SLAB), l0:l1] = (kn[:, l0:l1] + kr).astype(BF16)
        return carry

    lax.fori_loop(0, n_slab, prep_body, 0)

    ki = lax.broadcasted_iota(jnp.int32, (MLA_TQ, MLA_TQ), 0)
    qi_ = lax.broadcasted_iota(jnp.int32, (MLA_TQ, MLA_TQ), 1)
    diag_mask = (ki // CHUNK) <= (qi_ // CHUNK)

    def q_body(qi, carry):
        q0 = pl.multiple_of(qi * MLA_TQ, MLA_TQ)

        def attend(kj, state, masked):
            k0 = pl.multiple_of(kj * MLA_TQ, MLA_TQ)
            scores = [_dot(k_s[pl.ds(k0, MLA_TQ), hh * HEAD_PAD:(hh + 1) * HEAD_PAD],
                           qt_s[qi, hh * HEAD_PAD:(hh + 1) * HEAD_PAD, :])
                      for hh in range(MLA_HEADS)]
            stats, probs = [], []
            for hh, s in enumerate(scores):
                m, l, _ = state[hh]
                if masked:
                    s = jnp.where(diag_mask, s, -jnp.inf)
                m_new = jnp.maximum(m, jnp.max(s, axis=0, keepdims=True))
                alpha = jnp.exp(m - m_new)
                p = jnp.exp(s - m_new)
                stats.append((m_new, alpha * l + jnp.sum(p, axis=0, keepdims=True), alpha))
                probs.append(p.astype(BF16))
            return tuple(
                (m_new, l, alpha * state[hh][2] + _dot(vt_s[kj, hh * MLA_V:(hh + 1) * MLA_V, :], probs[hh]))
                for hh, (m_new, l, alpha) in enumerate(stats))

        init = tuple((jnp.full((1, MLA_TQ), -jnp.inf, F32), jnp.zeros((1, MLA_TQ), F32),
                      jnp.zeros((MLA_V, MLA_TQ), F32)) for _ in range(MLA_HEADS))
        state = attend(qi, init, True)
        state = lax.fori_loop(0, qi, lambda kj, st: attend(kj, st, False), state)
        out_t = jnp.concatenate([acc / l for (_, l, acc) in state], axis=0)
        o_ref[0, pl.ds(q0, MLA_TQ), :] = out_t.T.astype(o_ref.dtype)
        return carry

    lax.fori_loop(0, n_qblk, q_body, 0)


def _mla_weights(w_qb, w_kvb):
    qr = w_qb.reshape(MLA_Q_RANK, MLA_HEADS, MLA_NOPE + MLA_ROPE)
    zq = jnp.zeros((MLA_Q_RANK, MLA_HEADS, HEAD_PAD - MLA_NOPE - MLA_ROPE), w_qb.dtype)
    wq = jnp.concatenate([qr, zq], axis=2).reshape(MLA_Q_RANK, MLA_HEADS * HEAD_PAD)
    kvr = w_kvb.reshape(MLA_KV_RANK, MLA_HEADS, MLA_NOPE + MLA_V)
    zk = jnp.zeros((MLA_KV_RANK, MLA_HEADS, HEAD_PAD - MLA_NOPE), w_kvb.dtype)
    wk = jnp.concatenate([kvr[:, :, :MLA_NOPE], zk], axis=2).reshape(MLA_KV_RANK, MLA_HEADS * HEAD_PAD)
    wvt = kvr[:, :, MLA_NOPE:].reshape(MLA_KV_RANK, MLA_HEADS * MLA_V).T
    return wq.T.astype(BF16), wk.astype(BF16), wvt.astype(BF16)


def _mla_mixer(proj, positions, q_norm_g, w_qb, kv_norm_g, w_kvb):
    bsz, seq, _ = proj.shape
    inv_freq = ROPE_THETA ** (-jnp.arange(0, MLA_ROPE, 2, dtype=F32) / MLA_ROPE)
    invf = inv_freq.reshape(PE_HALF, 1)
    wqt, wk, wvt = _mla_weights(w_qb, w_kvb)
    wide = MLA_HEADS * HEAD_PAD
    vw = MLA_HEADS * MLA_V
    row = lambda shape: pl.BlockSpec(shape, lambda b: (0,) * len(shape))
    return pl.pallas_call(
        _mla_kernel,
        grid=(bsz,),
        in_specs=[pl.BlockSpec((1, seq, MLA_Q_RANK), lambda b: (b, 0, COL_MQ // MLA_Q_RANK)),
                  pl.BlockSpec((1, seq, MLA_KV_RANK), lambda b: (b, 0, COL_MKV // MLA_KV_RANK)),
                  pl.BlockSpec((1, seq, LANES), lambda b: (b, 0, COL_MISC // LANES)),
                  pl.BlockSpec((1, seq // SLAB, 1, SLAB), lambda b: (b, 0, 0, 0)),
                  row((PE_HALF, 1)), row((1, MLA_Q_RANK)), row((1, MLA_KV_RANK)),
                  row((wide, MLA_Q_RANK)), row((MLA_KV_RANK, wide)), row((vw, MLA_KV_RANK))],
        out_specs=pl.BlockSpec((1, seq, vw), lambda b: (b, 0, 0)),
        out_shape=jax.ShapeDtypeStruct((bsz, seq, vw), BF16),
        scratch_shapes=[pltpu.VMEM((seq // SLAB, wide, SLAB), BF16), pltpu.VMEM((seq, wide), BF16),
                        pltpu.VMEM((seq // SLAB, vw, SLAB), BF16)],
        compiler_params=_cparams(1),
        name="mla_mixer",
    )(proj, proj, proj, positions.reshape(bsz, seq // SLAB, 1, SLAB), invf,
      q_norm_g.astype(F32).reshape(1, MLA_Q_RANK), kv_norm_g.astype(F32).reshape(1, MLA_KV_RANK), wqt, wk, wvt)


FF_TILE = 1024


def _outmlp_kernel(h_ref, oa_ref, ob_ref, oc_ref, gtm_ref, shf_ref, scf_ref, gtf_ref, g_ref, fg_ref,
                   wo_ref, w1_ref, w2_ref, o_ref, *, final_norm):
    na, nb = oa_ref.shape[2], ob_ref.shape[2]
    mix = (_dot(oa_ref[0], wo_ref[0:na, :]) + _dot(ob_ref[0], wo_ref[na:na + nb, :])
           + _dot(oc_ref[0], wo_ref[na + nb:, :]))
    h1 = h_ref[0] + gtm_ref[0] * mix
    u = _modnorm(h1, g_ref[...], scf_ref[0], shf_ref[0]).astype(BF16)
    d_ff = w1_ref.shape[1]
    f = jnp.zeros_like(h1)
    for t in range(d_ff // FF_TILE):
        hid = jnp.maximum(_dot(u, w1_ref[:, t * FF_TILE:(t + 1) * FF_TILE]), 0.0)
        f = f + _dot((hid * hid).astype(BF16), w2_ref[t * FF_TILE:(t + 1) * FF_TILE, :])
    h2 = h1 + gtf_ref[0] * f
    if final_norm:
        ms = jnp.mean(h2 * h2, axis=-1, keepdims=True)
        h2 = h2 * lax.rsqrt(ms + EPS) * fg_ref[...]
    o_ref[0] = h2


def _out_mlp(h, o_a, o_b, o_c, mod_l, g, final_g, w_out, w1, w2, final_norm, tm=512):
    bsz, seq, d = h.shape
    d_ff = w1.shape[1]
    mod3 = mod_l.reshape(bsz, 1, N_MOD * d)
    modspec = lambda m: pl.BlockSpec((1, 1, d), lambda b, i: (b, 0, m))
    tile = lambda n: pl.BlockSpec((1, tm, n), lambda b, i: (b, i, 0))
    const = lambda shape: pl.BlockSpec(shape, lambda b, i: (0,) * len(shape), pipeline_mode=pl.Buffered(1))
    return pl.pallas_call(
        functools.partial(_outmlp_kernel, final_norm=final_norm),
        grid=(bsz, seq // tm),
        in_specs=[tile(d), tile(o_a.shape[2]), tile(o_b.shape[2]), tile(o_c.shape[2]),
                  modspec(2), modspec(3), modspec(4), modspec(5),
                  const((1, d)), const((1, d)),
                  const((d, d)), const((d, d_ff)), const((d_ff, d))],
        out_specs=tile(d),
        out_shape=jax.ShapeDtypeStruct((bsz, seq, d), F32),
        compiler_params=_cparams(2),
        name="outproj_mlp",
    )(h, o_a, o_b, o_c, mod3, mod3, mod3, mod3, g.reshape(1, d), final_g.reshape(1, d), w_out, w1, w2)


def _pad_w_in(w_in_l):
    d = w_in_l.shape[0]
    o = 0
    parts = {}
    for name, n in (("gqkvz", 4 * GDN_W), ("ga", GDN_HEADS), ("gb", GDN_HEADS), ("rx", RG_WIDTH),
                    ("rgate", RG_WIDTH), ("mq", MLA_Q_RANK), ("mkv", MLA_KV_RANK), ("mkr", MLA_ROPE)):
        parts[name] = w_in_l[:, o:o + n]
        o += n
    pad = jnp.zeros((d, NP_IN - COL_MISC - MLA_ROPE - 2 * GDN_HEADS), w_in_l.dtype)
    return jnp.concatenate([parts["gqkvz"], parts["rx"], parts["rgate"], parts["mq"], parts["mkv"],
                            parts["mkr"], parts["ga"], parts["gb"], pad], axis=1).astype(BF16)


def kernel(x, c, positions, w_mod, b_mod, norm_mix_g, w_in, gdn_conv_w, gdn_a_log, gdn_dt_bias, gdn_norm_g, rg_conv_w, rg_conv_b, rg_w_a, rg_b_a, rg_w_x, rg_b_x, rg_lambda, mla_q_norm_g, mla_w_qb, mla_kv_norm_g, mla_w_kvb, w_out, norm_mlp_g, w_mlp_in, w_mlp_out, final_norm_g):
    depth = w_mod.shape[0]
    mod = _modulation(c, w_mod, b_mod)
    h = x
    for l in range(depth):
        proj = _in_projection(h, mod[l], norm_mix_g[l], _pad_w_in(w_in[l]))
        o_a = _gdn_mixer(proj, gdn_conv_w[l], gdn_a_log[l], gdn_dt_bias[l], gdn_norm_g[l])
        o_b = _rg_mixer(proj, rg_conv_w[l], rg_conv_b[l], rg_w_a[l], rg_b_a[l], rg_w_x[l], rg_b_x[l],
                        rg_lambda[l])
        o_c = _mla_mixer(proj, positions, mla_q_norm_g[l], mla_w_qb[l], mla_kv_norm_g[l], mla_w_kvb[l])
        h = _out_mlp(h, o_a, o_b, o_c, mod[l], norm_mlp_g[l], final_norm_g,
                     w_out[l].astype(BF16), w_mlp_in[l].astype(BF16), w_mlp_out[l].astype(BF16),
                     final_norm=(l == depth - 1))
    return h
```

```python
import functools

import jax
import jax.numpy as jnp
from jax import lax
from jax.experimental import pallas as pl
from jax.experimental.pallas import tpu as pltpu

F32 = jnp.float32
BF16 = jnp.bfloat16

EPS = 1e-6
N_MOD = 6
CHUNK = 64
GDN_HEADS = 4
GDN_DK = 64
GDN_W = GDN_HEADS * GDN_DK
GDN_CONV = 4
RG_WIDTH = 512
RG_BLOCK = 64
RG_CONV = 4
RG_C = 8.0
MLA_HEADS = 4
MLA_NOPE = 64
MLA_ROPE = 32
MLA_V = 64
MLA_Q_RANK = 256
MLA_KV_RANK = 128
ROPE_THETA = 10000.0
LANES = 128
SUBLANES = 8
HEAD_PAD = LANES

COL_GDN = 0
COL_RX = 1024
COL_RGATE = 1536
COL_MQ = 2048
COL_MKV = 2304
COL_MISC = 2432
NP_IN = 2560
MISC_A = MLA_ROPE
MISC_B = MLA_ROPE + GDN_HEADS

SLAB = 256
VMEM_LIMIT = 56 * 1024 * 1024


def _cparams(n_axes):
    return pltpu.CompilerParams(dimension_semantics=("arbitrary",) * n_axes,
                                vmem_limit_bytes=VMEM_LIMIT)


def _dot(a, b):
    return jnp.dot(a, b, preferred_element_type=F32)


def _dot_nt(a, b):
    return lax.dot_general(a, b, (((1,), (1,)), ((), ())), preferred_element_type=F32)


def _split3(x):
    hi = x.astype(BF16)
    r1 = x - hi.astype(F32)
    mid = r1.astype(BF16)
    lo = (r1 - mid.astype(F32)).astype(BF16)
    return hi, mid, lo


def _dot3_lhs(x, w):
    hi, mid, lo = _split3(x)
    return _dot(hi, w) + _dot(mid, w) + _dot(lo, w)


def _dot3_rhs(w, x):
    hi, mid, lo = _split3(x)
    return _dot(w, hi) + _dot(w, mid) + _dot(w, lo)


def _sigmoid(x):
    return 1.0 / (1.0 + jnp.exp(-x))


def _silu(x):
    return x * _sigmoid(x)


def _softplus(x):
    return jnp.maximum(x, 0.0) + jnp.log1p(jnp.exp(-jnp.abs(x)))


def _conv_taps(prev, cur, w):
    taps = w.shape[0]
    xcat = jnp.concatenate([prev, cur], axis=0)
    acc = cur * w[taps - 1:taps, :]
    for j in range(taps - 1):
        shifted = pltpu.roll(xcat, taps - 1 - j, axis=0)[SUBLANES:, :]
        acc = acc + shifted * w[j:j + 1, :]
    return acc


def _causal_conv(ref, r0, rows, col0, ncols, w):
    cur = ref[0, pl.ds(r0, rows), col0:col0 + ncols]
    pstart = pl.multiple_of(jnp.maximum(r0 - SUBLANES, 0), SUBLANES)
    prev = ref[0, pl.ds(pstart, SUBLANES), col0:col0 + ncols]
    return _conv_taps(jnp.where(r0 > 0, prev, 0.0), cur, w)


def _mod_kernel(c_ref, w_ref, b_ref, o_ref):
    ca = _silu(c_ref[...])
    o_ref[0] = _dot(ca.astype(BF16), w_ref[0].astype(BF16)) + b_ref[0]


def _modulation(c, w_mod, b_mod):
    depth, d, n = w_mod.shape
    bsz = c.shape[0]
    tn = 1536
    return pl.pallas_call(
        _mod_kernel,
        grid=(depth, n // tn),
        in_specs=[pl.BlockSpec((bsz, d), lambda l, j: (0, 0)),
                  pl.BlockSpec((1, d, tn), lambda l, j: (l, 0, j)),
                  pl.BlockSpec((1, 1, tn), lambda l, j: (l, 0, j))],
        out_specs=pl.BlockSpec((1, bsz, tn), lambda l, j: (l, 0, j)),
        out_shape=jax.ShapeDtypeStruct((depth, bsz, n), F32),
        compiler_params=_cparams(2),
        name="adaln_mod",
    )(c, w_mod, b_mod.reshape(depth, 1, n))


def _modnorm(x, g, sc, sh):
    ms = jnp.mean(x * x, axis=-1, keepdims=True)
    return (x * lax.rsqrt(ms + EPS) * g) * (1.0 + sc) + sh


def _gelu_tanh(x):
    return 0.5 * x * (1.0 + jnp.tanh(0.7978845608028654 * (x + 0.044715 * (x * x * x))))


def _rg_gates(xc, w_ref):
    return [_dot(xc[:, gi * LANES:(gi + 1) * LANES].astype(BF16), w_ref[gi]) for gi in range(RG_WIDTH // LANES)]


def _rg_scan(xc_all, pres, gate, ba_ref, bx_ref, lam_ref, hc_s):
    n_tile = SLAB // SUBLANES
    sub = lax.broadcasted_iota(jnp.int32, (n_tile, SUBLANES, LANES), 1)
    outs = []
    for gi, pre in enumerate(pres):
        l0, l1 = gi * LANES, (gi + 1) * LANES
        xc = xc_all[:, l0:l1]
        r = _sigmoid(pre[:, :LANES] + ba_ref[:, l0:l1])
        ig = _sigmoid(pre[:, LANES:] + bx_ref[:, l0:l1])
        log_a = (-RG_C) * r * _softplus(-lam_ref[:, l0:l1])
        a = jnp.exp(log_a)
        om = 1.0 - a * a
        mult = jnp.where(om > 0.0, om * lax.rsqrt(om), 0.0)
        bt = mult * (ig * xc)
        a3 = a.reshape(n_tile, SUBLANES, LANES)
        b3 = bt.reshape(n_tile, SUBLANES, LANES)
        d = 1
        while d < SUBLANES:
            a_sh = pltpu.roll(a3, d, axis=1)
            b_sh = pltpu.roll(b3, d, axis=1)
            keep = sub >= d
            b3 = jnp.where(keep, a3 * b_sh + b3, b3)
            a3 = jnp.where(keep, a3 * a_sh, a3)
            d *= 2
        car = hc_s[:, l0:l1]
        hs = []
        for n in range(n_tile):
            hn = a3[n] * car + b3[n]
            hs.append(hn)
            car = jnp.broadcast_to(hn[SUBLANES - 1:SUBLANES, :], (SUBLANES, LANES))
        hc_s[:, l0:l1] = car
        outs.append(jnp.concatenate(hs, axis=0) * _gelu_tanh(gate[:, l0:l1]))
    return outs


def _inproj_kernel(h_ref, sh_ref, sc_ref, g_ref, w_ref, o_ref):
    u = _modnorm(h_ref[0], g_ref[...], sc_ref[0], sh_ref[0])
    o_ref[0] = _dot(u.astype(BF16), w_ref[...])


def _in_projection(h, mod_l, g, w_pad, tm=512):
    bsz, seq, d = h.shape
    npad = w_pad.shape[1]
    mod3 = mod_l.reshape(bsz, 1, N_MOD * d)
    return pl.pallas_call(
        _inproj_kernel,
        grid=(bsz, seq // tm),
        in_specs=[pl.BlockSpec((1, tm, d), lambda b, i: (b, i, 0)),
                  pl.BlockSpec((1, 1, d), lambda b, i: (b, 0, 0)),
                  pl.BlockSpec((1, 1, d), lambda b, i: (b, 0, 1)),
                  pl.BlockSpec((1, d), lambda b, i: (0, 0)),
                  pl.BlockSpec((d, npad), lambda b, i: (0, 0))],
        out_specs=pl.BlockSpec((1, tm, npad), lambda b, i: (b, i, 0)),
        out_shape=jax.ShapeDtypeStruct((bsz, seq, npad), F32),
        compiler_params=_cparams(2),
        name="norm_inproj",
    )(h, mod3, mod3, g.reshape(1, d), w_pad)


def _head_of(idx):
    return idx // GDN_DK


def _blockdiag(x):
    xb = x.astype(BF16)
    lane_head = _head_of(lax.broadcasted_iota(jnp.int32, xb.shape, 1))
    zero = jnp.zeros_like(xb)
    return jnp.concatenate([jnp.where(lane_head == hh, xb, zero) for hh in range(GDN_HEADS)], axis=0)


def _weave(main, side):
    result = None
    main_live = side_live = True
    while main_live or side_live:
        if side_live:
            try:
                next(side)
            except StopIteration:
                side_live = False
        if main_live:
            try:
                next(main)
            except StopIteration as stop:
                main_live, result = False, stop.value
    return result


def _gdn_kernel(qkvz_ref, misc_ref, convw_ref, nalog_ref, dtb_ref, ng_ref, o_ref, o_s, st_s):
    seq = qkvz_ref.shape[1]
    n_slab = seq // SLAB
    per_slab = SLAB // CHUNK

    ii = lax.broadcasted_iota(jnp.int32, (GDN_W, GDN_W), 0)
    jj = lax.broadcasted_iota(jnp.int32, (GDN_W, GDN_W), 1)
    same_blk = _head_of(ii) == _head_of(jj)
    ones_bd = jnp.where(same_blk, 1.0, 0.0).astype(BF16)
    tri_bd = jnp.where(same_blk & (ii >= jj), 1.0, 0.0).astype(BF16)
    mi = lax.broadcasted_iota(jnp.int32, (LANES, GDN_W), 0)
    mj = lax.broadcasted_iota(jnp.int32, (LANES, GDN_W), 1)
    sel_a = jnp.where(mi == MISC_A + _head_of(mj), 1.0, 0.0).astype(BF16)
    sel_b = jnp.where(mi == MISC_B + _head_of(mj), 1.0, 0.0).astype(BF16)
    ci = lax.broadcasted_iota(jnp.int32, (CHUNK, GDN_W), 0)
    cj = lax.broadcasted_iota(jnp.int32, (CHUNK, GDN_W), 1)
    cs = cj % CHUNK
    lower_incl = ci >= cs
    strict_lower = ci > cs
    eye_hl = jnp.where(ci == cs, 1.0, 0.0)
    first_lane = cs == 0
    ones_lhs = jnp.ones((CHUNK, GDN_W), BF16)
    convw = convw_ref[...]
    rng = [(c * CHUNK, (c + 1) * CHUNK) for c in range(per_slab)]

    def prepare(si):
        r0 = pl.multiple_of(si * SLAB, SLAB)
        y = _silu(_causal_conv(qkvz_ref, r0, SLAB, 0, 3 * GDN_W, convw))
        q, k, v = y[:, :GDN_W], y[:, GDN_W:2 * GDN_W], y[:, 2 * GDN_W:]

        def l2n(x):
            xx = x * x
            hi = xx.astype(BF16)
            lo = (xx - hi.astype(F32)).astype(BF16)
            ss = _dot(hi, ones_bd) + _dot(lo, ones_bd)
            return x * lax.rsqrt(ss + EPS)

        q = l2n(q) * (GDN_DK ** -0.5)
        k = l2n(k)
        yield
        misc = misc_ref[0, pl.ds(r0, SLAB), :]
        a_exp = _dot3_lhs(misc, sel_a)
        b_exp = _dot3_lhs(misc, sel_b)
        g = nalog_ref[...] * _softplus(a_exp + dtb_ref[...])
        beta = _sigmoid(b_exp)
        yield
        gcum = _dot3_rhs(tri_bd, g)
        glast = _dot3_rhs(ones_bd, g)
        eg = jnp.exp(gcum)
        kb = k * beta
        vb = v * beta
        kbg = kb * eg
        qd = (q * eg).astype(BF16)
        kt = k * jnp.exp(glast - gcum)
        cdec = jnp.exp(glast)
        yield
        aqs = [_dot_nt(jnp.concatenate([kb[a:b], q[a:b]], axis=0).astype(BF16), _blockdiag(k[a:b]))
               for a, b in rng]
        yield

        def row_of_cumsum(gc):
            gr = jnp.zeros((CHUNK, GDN_W), F32)
            for piece in _split3(jnp.where(first_lane, gc, 0.0)):
                gr = gr + _dot_nt(ones_lhs, _blockdiag(piece))
            return gr

        decays = [jnp.exp(jnp.where(lower_incl, gcum[a:b] - row_of_cumsum(gcum[a:b]), 0.0)) for a, b in rng]
        yield
        lmats = [jnp.where(strict_lower, aq[:CHUNK] * dec, 0.0) for aq, dec in zip(aqs, decays)]
        qks = [jnp.where(lower_incl, aq[CHUNK:] * dec, 0.0).astype(BF16) for aq, dec in zip(aqs, decays)]
        dinvs = [eye_hl - jnp.where(ci // 2 == cs // 2, lm, 0.0) for lm in lmats]
        blk = 2
        while blk < CHUNK:
            off_diag = (ci // (2 * blk) == cs // (2 * blk)) & (ci // blk != cs // blk)
            ps = [_dot(d.astype(BF16), _blockdiag(jnp.where(off_diag, lm, 0.0))) for d, lm in zip(dinvs, lmats)]
            yield
            dinvs = [d - _dot(p.astype(BF16), _blockdiag(d)) for d, p in zip(dinvs, ps)]
            yield
            blk *= 2
        ops = []
        for (a, b), d, qk in zip(rng, dinvs, qks):
            tb = d.astype(BF16)
            u = _dot(tb, _blockdiag(vb[a:b]))
            w = _dot(tb, _blockdiag(kbg[a:b])).astype(BF16)
            ktc = jnp.concatenate([kt[a:b], jnp.zeros((LANES - CHUNK, GDN_W), F32)], axis=0)
            ops.append((u, w, qd[a:b], qk, ktc.T.astype(BF16), cdec[a:a + 1]))
        return ops

    def recur(ops, r_base):
        for c, (u, w, qd, qk, ktt, cd) in enumerate(ops):
            wq = _dot(jnp.concatenate([w, qd], axis=0), st_s[...].astype(BF16))
            yield
            v_new = u - wq[:CHUNK]
            o_s[pl.ds(r_base + c * CHUNK, CHUNK), :] = wq[CHUNK:] + _dot(qk, _blockdiag(v_new))
            vpad = jnp.concatenate([v_new.astype(BF16), jnp.zeros((LANES - CHUNK, GDN_W), BF16)], axis=0)
            upd = _dot(ktt, vpad)
            yield
            st_s[...] = st_s[...] * cd + jnp.where(same_blk, upd, 0.0)
            yield

    def slab_body(si, prev_ops):
        r_prev = pl.multiple_of(jnp.maximum(si - 1, 0) * SLAB, SLAB)
        return _weave(prepare(si), recur(prev_ops, r_prev))

    idle = [(jnp.zeros((CHUNK, GDN_W), F32), jnp.zeros((CHUNK, GDN_W), BF16), jnp.zeros((CHUNK, GDN_W), BF16),
             jnp.zeros((CHUNK, GDN_W), BF16), jnp.zeros((GDN_W, LANES), BF16), jnp.ones((1, GDN_W), F32))
            for _ in range(per_slab)]
    st_s[...] = jnp.zeros_like(st_s)
    last_ops = lax.fori_loop(0, n_slab, slab_body, idle)
    for _ in recur(last_ops, (n_slab - 1) * SLAB):
        pass

    mean_bd = jnp.where(same_blk, 1.0 / GDN_DK, 0.0).astype(BF16)

    def out_body(si, carry):
        r0 = pl.multiple_of(si * SLAB, SLAB)
        o = o_s[pl.ds(r0, SLAB), :]
        oo = o * o
        hi = oo.astype(BF16)
        lo = (oo - hi.astype(F32)).astype(BF16)
        ms = _dot(hi, mean_bd) + _dot(lo, mean_bd)
        y = o * lax.rsqrt(ms + EPS) * ng_ref[...]
        z = qkvz_ref[0, pl.ds(r0, SLAB), 3 * GDN_W:4 * GDN_W]
        o_ref[0, pl.ds(r0, SLAB), :] = (y * _silu(z)).astype(o_ref.dtype)
        return carry

    lax.fori_loop(0, n_slab, out_body, 0)


def _gdn_mixer(proj, conv_w, a_log, dt_bias, norm_g):
    bsz, seq, _ = proj.shape
    nalog = jnp.repeat(-jnp.exp(a_log.astype(F32)), GDN_DK).reshape(1, GDN_W)
    dtb = jnp.repeat(dt_bias.astype(F32), GDN_DK).reshape(1, GDN_W)
    ng = jnp.tile(norm_g.astype(F32), GDN_HEADS).reshape(1, GDN_W)
    row = lambda shape: pl.BlockSpec(shape, lambda b: (0,) * len(shape))
    return pl.pallas_call(
        _gdn_kernel,
        grid=(bsz,),
        in_specs=[pl.BlockSpec((1, seq, 4 * GDN_W), lambda b: (b, 0, COL_GDN // (4 * GDN_W))),
                  pl.BlockSpec((1, seq, LANES), lambda b: (b, 0, COL_MISC // LANES)),
                  row((GDN_CONV, 3 * GDN_W)), row((1, GDN_W)), row((1, GDN_W)), row((1, GDN_W))],
        out_specs=pl.BlockSpec((1, seq, GDN_W), lambda b: (b, 0, 0)),
        out_shape=jax.ShapeDtypeStruct((bsz, seq, GDN_W), BF16),
        scratch_shapes=[pltpu.VMEM((seq, GDN_W), F32),
                        pltpu.VMEM((GDN_W, GDN_W), F32)],
        compiler_params=_cparams(1),
        name="gdn_mixer",
    )(proj, proj, conv_w, nalog, dtb, ng)


def _rg_weights(w_a, w_x):
    def pair_bd(w):
        z = jnp.zeros((RG_BLOCK, RG_BLOCK), w.dtype)
        out = []
        for gi in range(RG_WIDTH // LANES):
            top = jnp.concatenate([w[2 * gi], z], axis=1)
            bot = jnp.concatenate([z, w[2 * gi + 1]], axis=1)
            out.append(jnp.concatenate([top, bot], axis=0))
        return jnp.stack(out)
    return jnp.concatenate([pair_bd(w_a), pair_bd(w_x)], axis=2).astype(BF16)


MLA_TQ = SLAB
PE0 = MLA_NOPE
PE_HALF = MLA_ROPE // 2


def _rms(x, g):
    ms = jnp.mean(x * x, axis=-1, keepdims=True)
    return x * lax.rsqrt(ms + EPS) * g


def _mla_kernel(mq_ref, mkv_ref, misc_ref, pos_ref, invf_ref, qg_ref, kvg_ref, wqt_ref, wk_ref, wvt_ref,
                o_ref, qt_s, k_s, vt_s):
    seq = mq_ref.shape[1]
    n_slab = seq // SLAB
    n_qblk = seq // MLA_TQ
    scale = (MLA_NOPE + MLA_ROPE) ** -0.5
    x1_0, x2_0, pe_end = PE0, PE0 + PE_HALF, PE0 + MLA_ROPE

    def prep_body(si, carry):
        r0 = pl.multiple_of(si * SLAB, SLAB)
        ang_t = invf_ref[...] * pos_ref[0, si].astype(F32)
        cos_t = jnp.cos(ang_t)
        sin_t = jnp.sin(ang_t)

        def rope_t(x1, x2):
            return x1 * cos_t - x2 * sin_t, x2 * cos_t + x1 * sin_t

        qn = _rms(mq_ref[0, pl.ds(r0, SLAB), :], qg_ref[...]).astype(BF16)
        q_t = _dot_nt(wqt_ref[...], qn)
        for hh in range(MLA_HEADS):
            blk = q_t[hh * HEAD_PAD:(hh + 1) * HEAD_PAD]
            x1, x2 = rope_t(blk[x1_0:x2_0], blk[x2_0:pe_end])
            roped = jnp.concatenate([blk[:PE0], x1, x2, blk[pe_end:]], axis=0)
            qt_s[si, hh * HEAD_PAD:(hh + 1) * HEAD_PAD, :] = (roped * scale).astype(BF16)
        kvn = _rms(mkv_ref[0, pl.ds(r0, SLAB), :], kvg_ref[...]).astype(BF16)
        kn = _dot(kvn, wk_ref[...])
        vt_s[si] = _dot_nt(wvt_ref[...], kvn).astype(BF16)
        misc_t = misc_ref[0, pl.ds(r0, SLAB), :].T
        k1, k2 = rope_t(misc_t[:PE_HALF], misc_t[PE_HALF:MLA_ROPE])
        kr = jnp.concatenate([jnp.zeros((PE0, SLAB), F32), k1, k2,
                              jnp.zeros((HEAD_PAD - pe_end, SLAB), F32)], axis=0).T
        for hh in range(MLA_HEADS):
            l0, l1 = hh * HEAD_PAD, (hh + 1) * HEAD_PAD
            k_s[pl.ds(r0, SLAB), l0:l1] = (kn[:, l0:l1] + kr).astype(BF16)
        return carry

    lax.fori_loop(0, n_slab, prep_body, 0)

    ki = lax.broadcasted_iota(jnp.int32, (MLA_TQ, MLA_TQ), 0)
    qi_ = lax.broadcasted_iota(jnp.int32, (MLA_TQ, MLA_TQ), 1)
    diag_mask = (ki // CHUNK) <= (qi_ // CHUNK)

    def q_body(qi, carry):
        q0 = pl.multiple_of(qi * MLA_TQ, MLA_TQ)

        def attend(kj, state, masked):
            k0 = pl.multiple_of(kj * MLA_TQ, MLA_TQ)
            scores = [_dot(k_s[pl.ds(k0, MLA_TQ), hh * HEAD_PAD:(hh + 1) * HEAD_PAD],
                           qt_s[qi, hh * HEAD_PAD:(hh + 1) * HEAD_PAD, :])
                      for hh in range(MLA_HEADS)]
            stats, probs = [], []
            for hh, s in enumerate(scores):
                m, l, _ = state[hh]
                if masked:
                    s = jnp.where(diag_mask, s, -jnp.inf)
                m_new = jnp.maximum(m, jnp.max(s, axis=0, keepdims=True))
                alpha = jnp.exp(m - m_new)
                p = jnp.exp(s - m_new)
                stats.append((m_new, alpha * l + jnp.sum(p, axis=0, keepdims=True), alpha))
                probs.append(p.astype(BF16))
            return tuple(
                (m_new, l, alpha * state[hh][2] + _dot(vt_s[kj, hh * MLA_V:(hh + 1) * MLA_V, :], probs[hh]))
                for hh, (m_new, l, alpha) in enumerate(stats))

        init = tuple((jnp.full((1, MLA_TQ), -jnp.inf, F32), jnp.zeros((1, MLA_TQ), F32),
                      jnp.zeros((MLA_V, MLA_TQ), F32)) for _ in range(MLA_HEADS))
        state = attend(qi, init, True)
        state = lax.fori_loop(0, qi, lambda kj, st: attend(kj, st, False), state)
        out_t = jnp.concatenate([acc / l for (_, l, acc) in state], axis=0)
        o_ref[0, pl.ds(q0, MLA_TQ), :] = out_t.T.astype(o_ref.dtype)
        return carry

    lax.fori_loop(0, n_qblk, q_body, 0)


def _mla_weights(w_qb, w_kvb):
    qr = w_qb.reshape(MLA_Q_RANK, MLA_HEADS, MLA_NOPE + MLA_ROPE)
    zq = jnp.zeros((MLA_Q_RANK, MLA_HEADS, HEAD_PAD - MLA_NOPE - MLA_ROPE), w_qb.dtype)
    wq = jnp.concatenate([qr, zq], axis=2).reshape(MLA_Q_RANK, MLA_HEADS * HEAD_PAD)
    kvr = w_kvb.reshape(MLA_KV_RANK, MLA_HEADS, MLA_NOPE + MLA_V)
    zk = jnp.zeros((MLA_KV_RANK, MLA_HEADS, HEAD_PAD - MLA_NOPE), w_kvb.dtype)
    wk = jnp.concatenate([kvr[:, :, :MLA_NOPE], zk], axis=2).reshape(MLA_KV_RANK, MLA_HEADS * HEAD_PAD)
    wvt = kvr[:, :, MLA_NOPE:].reshape(MLA_KV_RANK, MLA_HEADS * MLA_V).T
    return wq.T.astype(BF16), wk.astype(BF16), wvt.astype(BF16)


def _mla_mixer(proj, positions, q_norm_g, w_qb, kv_norm_g, w_kvb):
    bsz, seq, _ = proj.shape
    inv_freq = ROPE_THETA ** (-jnp.arange(0, MLA_ROPE, 2, dtype=F32) / MLA_ROPE)
    invf = inv_freq.reshape(PE_HALF, 1)
    wqt, wk, wvt = _mla_weights(w_qb, w_kvb)
    wide = MLA_HEADS * HEAD_PAD
    vw = MLA_HEADS * MLA_V
    row = lambda shape: pl.BlockSpec(shape, lambda b: (0,) * len(shape))
    return pl.pallas_call(
        _mla_kernel,
        grid=(bsz,),
        in_specs=[pl.BlockSpec((1, seq, MLA_Q_RANK), lambda b: (b, 0, COL_MQ // MLA_Q_RANK)),
                  pl.BlockSpec((1, seq, MLA_KV_RANK), lambda b: (b, 0, COL_MKV // MLA_KV_RANK)),
                  pl.BlockSpec((1, seq, LANES), lambda b: (b, 0, COL_MISC // LANES)),
                  pl.BlockSpec((1, seq // SLAB, 1, SLAB), lambda b: (b, 0, 0, 0)),
                  row((PE_HALF, 1)), row((1, MLA_Q_RANK)), row((1, MLA_KV_RANK)),
                  row((wide, MLA_Q_RANK)), row((MLA_KV_RANK, wide)), row((vw, MLA_KV_RANK))],
        out_specs=pl.BlockSpec((1, seq, vw), lambda b: (b, 0, 0)),
        out_shape=jax.ShapeDtypeStruct((bsz, seq, vw), BF16),
        scratch_shapes=[pltpu.VMEM((seq // SLAB, wide, SLAB), BF16), pltpu.VMEM((seq, wide), BF16),
                        pltpu.VMEM((seq // SLAB, vw, SLAB), BF16)],
        compiler_params=_cparams(1),
        name="mla_mixer",
    )(proj, proj, proj, positions.reshape(bsz, seq // SLAB, 1, SLAB), invf,
      q_norm_g.astype(F32).reshape(1, MLA_Q_RANK), kv_norm_g.astype(F32).reshape(1, MLA_KV_RANK), wqt, wk, wvt)


FF_TILE = 1024


def _order_after(x, anchor):
    zero_bits = lax.shift_right_logical(pltpu.bitcast(anchor, jnp.uint32), jnp.uint32(32))
    return x + pltpu.bitcast(zero_bits, F32)


def _rg_front(rx, tail, convw_ref, convb_ref, wrg_ref, xc_s, pre_s):
    convw = convw_ref[...]
    for s0 in range(0, rx.shape[0], SLAB):
        prev = tail if s0 == 0 else rx[s0 - SUBLANES:s0]
        xc = _conv_taps(prev, rx[s0:s0 + SLAB], convw) + convb_ref[...]
        xc_s[s0:s0 + SLAB, :] = xc
        for gi, pre in enumerate(_rg_gates(xc, wrg_ref)):
            pre_s[s0:s0 + SLAB, gi * 2 * LANES:(gi + 1) * 2 * LANES] = pre


def _rg_back(xc_s, pre_s, gate, anchor, ba_ref, bx_ref, lam_ref, hc_s, ob_s):
    for s0 in range(0, xc_s.shape[0], SLAB):
        pres = [pre_s[s0:s0 + SLAB, gi * 2 * LANES:(gi + 1) * 2 * LANES] for gi in range(RG_WIDTH // LANES)]
        if anchor is not None:
            pres = [_order_after(p, anchor) for p in pres]
        outs = _rg_scan(xc_s[s0:s0 + SLAB, :], pres, gate[s0:s0 + SLAB], ba_ref, bx_ref, lam_ref, hc_s)
        for gi, o in enumerate(outs):
            ob_s[s0:s0 + SLAB, gi * LANES:(gi + 1) * LANES] = o.astype(ob_s.dtype)


def _outmlp_kernel(h_ref, oa_ref, oc_ref, gtm_ref, shf_ref, scf_ref, gtf_ref, g_ref, fg_ref,
                   wo_ref, w1_ref, w2_ref,
                   rx0_ref, gate0_ref, rxn_ref, gaten_ref, tailn_ref,
                   convw_ref, convb_ref, wrg_ref, ba_ref, bx_ref, lam_ref,
                   o_ref, ob_s, hc_s, xc_s, pre_s, *, final_norm):
    n_i = pl.num_programs(1)
    step = pl.program_id(0) * n_i + pl.program_id(1)
    front_refs = (convw_ref, convb_ref, wrg_ref, xc_s, pre_s)
    back_refs = (ba_ref, bx_ref, lam_ref, hc_s, ob_s)

    @pl.when(step == 0)
    def _():
        hc_s[...] = jnp.zeros_like(hc_s)
        _rg_front(rx0_ref[0], jnp.zeros((SUBLANES, RG_WIDTH), F32), *front_refs)
        _rg_back(xc_s, pre_s, gate0_ref[0], None, *back_refs)

    seq_head = (step + 1) % n_i == 0
    hc_s[...] = jnp.where(seq_head, 0.0, hc_s[...])
    _rg_front(rxn_ref[0], jnp.where(seq_head, 0.0, tailn_ref[0]), *front_refs)
    na, nb = oa_ref.shape[2], ob_s.shape[1]
    mix = (_dot(oa_ref[0], wo_ref[0:na, :]) + _dot(ob_s[...], wo_ref[na:na + nb, :])
           + _dot(oc_ref[0], wo_ref[na + nb:, :]))
    h1 = h_ref[0] + gtm_ref[0] * mix
    u = _modnorm(h1, g_ref[...], scf_ref[0], shf_ref[0]).astype(BF16)
    d_ff = w1_ref.shape[1]
    f = jnp.zeros_like(h1)
    anchor = None
    for t in range(d_ff // FF_TILE):
        hid = jnp.maximum(_dot(u, w1_ref[:, t * FF_TILE:(t + 1) * FF_TILE]), 0.0)
        if t == 0:
            anchor = hid[0:1, 0:2 * LANES]
        f = f + _dot((hid * hid).astype(BF16), w2_ref[t * FF_TILE:(t + 1) * FF_TILE, :])
    h2 = h1 + gtf_ref[0] * f
    if final_norm:
        ms = jnp.mean(h2 * h2, axis=-1, keepdims=True)
        h2 = h2 * lax.rsqrt(ms + EPS) * fg_ref[...]
    o_ref[0] = h2
    _rg_back(xc_s, pre_s, gaten_ref[0], anchor, *back_refs)


def _out_mlp(h, o_a, o_c, proj, mod_l, g, final_g, w_out, w1, w2, rg_params, final_norm, tm=512):
    bsz, seq, d = h.shape
    d_ff = w1.shape[1]
    n_i = seq // tm
    last = bsz * n_i - 1
    mod3 = mod_l.reshape(bsz, 1, N_MOD * d)
    conv_w, conv_b, w_a, b_a, w_x, b_x, lam = rg_params
    r2 = lambda v: v.astype(F32).reshape(1, RG_WIDTH)
    modspec = lambda m: pl.BlockSpec((1, 1, d), lambda b, i: (b, 0, m))
    tile = lambda n: pl.BlockSpec((1, tm, n), lambda b, i: (b, i, 0))
    const = lambda shape: pl.BlockSpec(shape, lambda b, i: (0,) * len(shape), pipeline_mode=pl.Buffered(1))

    def nxt(b, i):
        lin = jnp.minimum(b * n_i + i + 1, last)
        return lin // n_i, lin % n_i

    def next_tile(col):
        return pl.BlockSpec((1, tm, RG_WIDTH), lambda b, i: (*nxt(b, i), col // RG_WIDTH))

    def next_tail(b, i):
        nb, ni = nxt(b, i)
        return nb, jnp.maximum(ni * (tm // SUBLANES) - 1, 0), COL_RX // RG_WIDTH

    first_tile = lambda col: pl.BlockSpec((1, tm, RG_WIDTH), lambda b, i: (0, 0, col // RG_WIDTH))
    return pl.pallas_call(
        functools.partial(_outmlp_kernel, final_norm=final_norm),
        grid=(bsz, n_i),
        in_specs=[tile(d), tile(o_a.shape[2]), tile(o_c.shape[2]),
                  modspec(2), modspec(3), modspec(4), modspec(5),
                  const((1, d)), const((1, d)),
                  const((d, d)), const((d, d_ff)), const((d_ff, d)),
                  first_tile(COL_RX), first_tile(COL_RGATE), next_tile(COL_RX), next_tile(COL_RGATE),
                  pl.BlockSpec((1, SUBLANES, RG_WIDTH), next_tail),
                  const((RG_CONV, RG_WIDTH)), const((1, RG_WIDTH)),
                  const((RG_WIDTH // LANES, LANES, 2 * LANES)),
                  const((1, RG_WIDTH)), const((1, RG_WIDTH)), const((1, RG_WIDTH))],
        out_specs=tile(d),
        out_shape=jax.ShapeDtypeStruct((bsz, seq, d), F32),
        scratch_shapes=[pltpu.VMEM((tm, RG_WIDTH), BF16),
                        pltpu.VMEM((SUBLANES, RG_WIDTH), F32),
                        pltpu.VMEM((tm, RG_WIDTH), F32),
                        pltpu.VMEM((tm, 2 * RG_WIDTH), F32)],
        compiler_params=_cparams(2),
        name="outproj_mlp_rglru",
    )(h, o_a, o_c, mod3, mod3, mod3, mod3, g.reshape(1, d), final_g.reshape(1, d), w_out, w1, w2,
      proj, proj, proj, proj, proj,
      conv_w, r2(conv_b), _rg_weights(w_a, w_x), r2(b_a), r2(b_x), r2(lam))


def _pad_w_in(w_in_l):
    d = w_in_l.shape[0]
    o = 0
    parts = {}
    for name, n in (("gqkvz", 4 * GDN_W), ("ga", GDN_HEADS), ("gb", GDN_HEADS), ("rx", RG_WIDTH),
                    ("rgate", RG_WIDTH), ("mq", MLA_Q_RANK), ("mkv", MLA_KV_RANK), ("mkr", MLA_ROPE)):
        parts[name] = w_in_l[:, o:o + n]
        o += n
    pad = jnp.zeros((d, NP_IN - COL_MISC - MLA_ROPE - 2 * GDN_HEADS), w_in_l.dtype)
    return jnp.concatenate([parts["gqkvz"], parts["rx"], parts["rgate"], parts["mq"], parts["mkv"],
                            parts["mkr"], parts["ga"], parts["gb"], pad], axis=1).astype(BF16)


def kernel(x, c, positions, w_mod, b_mod, norm_mix_g, w_in, gdn_conv_w, gdn_a_log, gdn_dt_bias, gdn_norm_g, rg_conv_w, rg_conv_b, rg_w_a, rg_b_a, rg_w_x, rg_b_x, rg_lambda, mla_q_norm_g, mla_w_qb, mla_kv_norm_g, mla_w_kvb, w_out, norm_mlp_g, w_mlp_in, w_mlp_out, final_norm_g):
    depth = w_mod.shape[0]
    mod = _modulation(c, w_mod, b_mod)
    h = x
    for l in range(depth):
        rg_params = (rg_conv_w[l], rg_conv_b[l], rg_w_a[l], rg_b_a[l], rg_w_x[l], rg_b_x[l], rg_lambda[l])
        proj = _in_projection(h, mod[l], norm_mix_g[l], _pad_w_in(w_in[l]))
        o_a = _gdn_mixer(proj, gdn_conv_w[l], gdn_a_log[l], gdn_dt_bias[l], gdn_norm_g[l])
        o_c = _mla_mixer(proj, positions, mla_q_norm_g[l], mla_w_qb[l], mla_kv_norm_g[l], mla_w_kvb[l])
        h = _out_mlp(h, o_a, o_c, proj, mod[l], norm_mlp_g[l], final_norm_g,
                     w_out[l].astype(BF16), w_mlp_in[l].astype(BF16), w_mlp_out[l].astype(BF16),
                     rg_params, final_norm=(l == depth - 1))
    return h
```

```python
import functools

import jax
import jax.numpy as jnp
from jax import lax
from jax.experimental import pallas as pl
from jax.experimental.pallas import tpu as pltpu

F32 = jnp.float32
BF16 = jnp.bfloat16

EPS = 1e-6
N_MOD = 6
CHUNK = 64
GDN_HEADS = 4
GDN_DK = 64
GDN_W = GDN_HEADS * GDN_DK
GDN_CONV = 4
RG_WIDTH = 512
RG_BLOCK = 64
RG_CONV = 4
RG_C = 8.0
MLA_HEADS = 4
MLA_NOPE = 64
MLA_ROPE = 32
MLA_V = 64
MLA_Q_RANK = 256
MLA_KV_RANK = 128
ROPE_THETA = 10000.0
LANES = 128
SUBLANES = 8
HEAD_PAD = LANES

COL_GDN = 0
COL_RX = 1024
COL_RGATE = 1536
COL_MQ = 2048
COL_MKV = 2304
COL_MISC = 2432
NP_IN = 2560
MISC_A = MLA_ROPE
MISC_B = MLA_ROPE + GDN_HEADS

SLAB = 256
VMEM_LIMIT = 56 * 1024 * 1024


def _cparams(n_axes):
    return pltpu.CompilerParams(dimension_semantics=("arbitrary",) * n_axes,
                                vmem_limit_bytes=VMEM_LIMIT)


def _dot(a, b):
    return jnp.dot(a, b, preferred_element_type=F32)


def _dot_nt(a, b):
    return lax.dot_general(a, b, (((1,), (1,)), ((), ())), preferred_element_type=F32)


def _split3(x):
    hi = x.astype(BF16)
    r1 = x - hi.astype(F32)
    mid = r1.astype(BF16)
    lo = (r1 - mid.astype(F32)).astype(BF16)
    return hi, mid, lo


def _dot3_lhs(x, w):
    hi, mid, lo = _split3(x)
    return _dot(hi, w) + _dot(mid, w) + _dot(lo, w)


def _dot3_rhs(w, x):
    hi, mid, lo = _split3(x)
    return _dot(w, hi) + _dot(w, mid) + _dot(w, lo)


def _sigmoid(x):
    return 1.0 / (1.0 + jnp.exp(-x))


def _silu(x):
    return x * _sigmoid(x)


def _softplus(x):
    return jnp.maximum(x, 0.0) + jnp.log1p(jnp.exp(-jnp.abs(x)))


def _conv_taps(prev, cur, w):
    taps = w.shape[0]
    xcat = jnp.concatenate([prev, cur], axis=0)
    acc = cur * w[taps - 1:taps, :]
    for j in range(taps - 1):
        shifted = pltpu.roll(xcat, taps - 1 - j, axis=0)[SUBLANES:, :]
        acc = acc + shifted * w[j:j + 1, :]
    return acc


def _causal_conv(ref, r0, rows, col0, ncols, w):
    cur = ref[0, pl.ds(r0, rows), col0:col0 + ncols]
    pstart = pl.multiple_of(jnp.maximum(r0 - SUBLANES, 0), SUBLANES)
    prev = ref[0, pl.ds(pstart, SUBLANES), col0:col0 + ncols]
    return _conv_taps(jnp.where(r0 > 0, prev, 0.0), cur, w)


def _mod_kernel(c_ref, w_ref, b_ref, o_ref):
    ca = _silu(c_ref[...])
    o_ref[0] = _dot(ca.astype(BF16), w_ref[0].astype(BF16)) + b_ref[0]


def _modulation(c, w_mod, b_mod):
    depth, d, n = w_mod.shape
    bsz = c.shape[0]
    tn = 1536
    return pl.pallas_call(
        _mod_kernel,
        grid=(depth, n // tn),
        in_specs=[pl.BlockSpec((bsz, d), lambda l, j: (0, 0)),
                  pl.BlockSpec((1, d, tn), lambda l, j: (l, 0, j)),
                  pl.BlockSpec((1, 1, tn), lambda l, j: (l, 0, j))],
        out_specs=pl.BlockSpec((1, bsz, tn), lambda l, j: (l, 0, j)),
        out_shape=jax.ShapeDtypeStruct((depth, bsz, n), F32),
        compiler_params=_cparams(2),
        name="adaln_mod",
    )(c, w_mod, b_mod.reshape(depth, 1, n))


def _modnorm(x, g, sc, sh):
    ms = jnp.mean(x * x, axis=-1, keepdims=True)
    return (x * lax.rsqrt(ms + EPS) * g) * (1.0 + sc) + sh


def _gelu_tanh(x):
    return 0.5 * x * (1.0 + jnp.tanh(0.7978845608028654 * (x + 0.044715 * (x * x * x))))


def _rg_gates(xc, w_ref):
    return [_dot(xc[:, gi * LANES:(gi + 1) * LANES].astype(BF16), w_ref[gi]) for gi in range(RG_WIDTH // LANES)]


def _rg_scan(xc_all, pres, gate, ba_ref, bx_ref, lam_ref, hc_s):
    n_tile = SLAB // SUBLANES
    sub = lax.broadcasted_iota(jnp.int32, (n_tile, SUBLANES, LANES), 1)
    outs = []
    for gi, pre in enumerate(pres):
        l0, l1 = gi * LANES, (gi + 1) * LANES
        xc = xc_all[:, l0:l1]
        r = _sigmoid(pre[:, :LANES] + ba_ref[:, l0:l1])
        ig = _sigmoid(pre[:, LANES:] + bx_ref[:, l0:l1])
        log_a = (-RG_C) * r * _softplus(-lam_ref[:, l0:l1])
        a = jnp.exp(log_a)
        om = 1.0 - a * a
        mult = jnp.where(om > 0.0, om * lax.rsqrt(om), 0.0)
        bt = mult * (ig * xc)
        a3 = a.reshape(n_tile, SUBLANES, LANES)
        b3 = bt.reshape(n_tile, SUBLANES, LANES)
        d = 1
        while d < SUBLANES:
            a_sh = pltpu.roll(a3, d, axis=1)
            b_sh = pltpu.roll(b3, d, axis=1)
            keep = sub >= d
            b3 = jnp.where(keep, a3 * b_sh + b3, b3)
            a3 = jnp.where(keep, a3 * a_sh, a3)
            d *= 2
        car = hc_s[:, l0:l1]
        hs = []
        for n in range(n_tile):
            hn = a3[n] * car + b3[n]
            hs.append(hn)
            car = jnp.broadcast_to(hn[SUBLANES - 1:SUBLANES, :], (SUBLANES, LANES))
        hc_s[:, l0:l1] = car
        outs.append(jnp.concatenate(hs, axis=0) * _gelu_tanh(gate[:, l0:l1]))
    return outs


def _inproj_kernel(h_ref, sh_ref, sc_ref, g_ref, w_ref, o_ref):
    u = _modnorm(h_ref[0], g_ref[...], sc_ref[0], sh_ref[0])
    o_ref[0] = _dot(u.astype(BF16), w_ref[...])


def _in_projection(h, mod_l, g, w_pad, tm=512):
    bsz, seq, d = h.shape
    npad = w_pad.shape[1]
    mod3 = mod_l.reshape(bsz, 1, N_MOD * d)
    return pl.pallas_call(
        _inproj_kernel,
        grid=(bsz, seq // tm),
        in_specs=[pl.BlockSpec((1, tm, d), lambda b, i: (b, i, 0)),
                  pl.BlockSpec((1, 1, d), lambda b, i: (b, 0, 0)),
                  pl.BlockSpec((1, 1, d), lambda b, i: (b, 0, 1)),
                  pl.BlockSpec((1, d), lambda b, i: (0, 0)),
                  pl.BlockSpec((d, npad), lambda b, i: (0, 0))],
        out_specs=pl.BlockSpec((1, tm, npad), lambda b, i: (b, i, 0)),
        out_shape=jax.ShapeDtypeStruct((bsz, seq, npad), F32),
        compiler_params=_cparams(2),
        name="norm_inproj",
    )(h, mod3, mod3, g.reshape(1, d), w_pad)


def _head_of(idx):
    return idx // GDN_DK


def _blockdiag(x):
    xb = x.astype(BF16)
    lane_head = _head_of(lax.broadcasted_iota(jnp.int32, xb.shape, 1))
    zero = jnp.zeros_like(xb)
    return jnp.concatenate([jnp.where(lane_head == hh, xb, zero) for hh in range(GDN_HEADS)], axis=0)


def _weave(*stages, start=None):
    start = start or [0] * len(stages)
    results = [None] * len(stages)
    live = list(range(len(stages)))
    turn = 0
    while live:
        for idx in list(live):
            if turn < start[idx]:
                continue
            try:
                next(stages[idx])
            except StopIteration as stop:
                results[idx] = stop.value
                live.remove(idx)
        turn += 1
    return results


def _gdn_kernel(qkvz_ref, misc_ref, convw_ref, nalog_ref, dtb_ref, ng_ref, o_ref, o_s, st_s):
    seq = qkvz_ref.shape[1]
    n_slab = seq // SLAB
    per_slab = SLAB // CHUNK

    ii = lax.broadcasted_iota(jnp.int32, (GDN_W, GDN_W), 0)
    jj = lax.broadcasted_iota(jnp.int32, (GDN_W, GDN_W), 1)
    same_blk = _head_of(ii) == _head_of(jj)
    ones_bd = jnp.where(same_blk, 1.0, 0.0).astype(BF16)
    tri_bd = jnp.where(same_blk & (ii >= jj), 1.0, 0.0).astype(BF16)
    mi = lax.broadcasted_iota(jnp.int32, (LANES, GDN_W), 0)
    mj = lax.broadcasted_iota(jnp.int32, (LANES, GDN_W), 1)
    sel_a = jnp.where(mi == MISC_A + _head_of(mj), 1.0, 0.0).astype(BF16)
    sel_b = jnp.where(mi == MISC_B + _head_of(mj), 1.0, 0.0).astype(BF16)
    ci = lax.broadcasted_iota(jnp.int32, (CHUNK, GDN_W), 0)
    cj = lax.broadcasted_iota(jnp.int32, (CHUNK, GDN_W), 1)
    cs = cj % CHUNK
    lower_incl = ci >= cs
    strict_lower = ci > cs
    eye_hl = jnp.where(ci == cs, 1.0, 0.0)
    convw = convw_ref[...]
    rng = [(c * CHUNK, (c + 1) * CHUNK) for c in range(per_slab)]

    def slab_stage(si):
        r0 = pl.multiple_of(si * SLAB, SLAB)
        y = _silu(_causal_conv(qkvz_ref, r0, SLAB, 0, 3 * GDN_W, convw))
        q, k, v = y[:, :GDN_W], y[:, GDN_W:2 * GDN_W], y[:, 2 * GDN_W:]

        def l2n(x):
            xx = x * x
            hi = xx.astype(BF16)
            lo = (xx - hi.astype(F32)).astype(BF16)
            ss = _dot(hi, ones_bd) + _dot(lo, ones_bd)
            return x * lax.rsqrt(ss + EPS)

        q = l2n(q) * (GDN_DK ** -0.5)
        k = l2n(k)
        yield
        misc = misc_ref[0, pl.ds(r0, SLAB), :]
        a_exp = _dot3_lhs(misc, sel_a)
        b_exp = _dot3_lhs(misc, sel_b)
        g = nalog_ref[...] * _softplus(a_exp + dtb_ref[...])
        beta = _sigmoid(b_exp)
        yield
        gcum = _dot3_rhs(tri_bd, g)
        glast = _dot3_rhs(ones_bd, g)
        eg = jnp.exp(gcum)
        kb = k * beta
        vb = (v * beta).astype(BF16)
        kbg = (kb * eg).astype(BF16)
        qd = (q * eg).astype(BF16)
        kt = k * jnp.exp(glast - gcum)
        cdec = jnp.exp(glast)
        ktts = []
        for a, b in rng:
            ktc = jnp.concatenate([kt[a:b], jnp.zeros((LANES - CHUNK, GDN_W), F32)], axis=0)
            ktts.append(ktc.T.astype(BF16))
        yield
        return (q.astype(BF16), k.astype(BF16), kb.astype(BF16), vb, kbg, qd, gcum, ktts,
                [cdec[a:a + 1] for a, _ in rng])

    def chunk_stage(arrs):
        q, k, kb, vb, kbg, qd, gcum, ktts, cds = arrs
        aqs = [_dot_nt(jnp.concatenate([kb[a:b], q[a:b]], axis=0), _blockdiag(k[a:b])) for a, b in rng]
        yield

        g_t = [gcum[:, :LANES].T, gcum[:, LANES:].T]
        head_rows = [g_t[hh // 2][(hh % 2) * GDN_DK:(hh % 2) * GDN_DK + 1, :] for hh in range(GDN_HEADS)]
        grs = [jnp.concatenate([row[:, a:b] for row in head_rows], axis=1) for a, b in rng]
        decays = [jnp.exp(jnp.where(lower_incl, gcum[a:b] - gr, 0.0)) for (a, b), gr in zip(rng, grs)]
        yield
        lmats = [jnp.where(strict_lower, aq[:CHUNK] * dec, 0.0) for aq, dec in zip(aqs, decays)]
        qks = [jnp.where(lower_incl, aq[CHUNK:] * dec, 0.0).astype(BF16) for aq, dec in zip(aqs, decays)]
        dinvs = [eye_hl - jnp.where(ci // 2 == cs // 2, lm, 0.0) for lm in lmats]
        blk = 2
        while blk < CHUNK:
            off_diag = (ci // (2 * blk) == cs // (2 * blk)) & (ci // blk != cs // blk)
            ps = [_dot(d.astype(BF16), _blockdiag(jnp.where(off_diag, lm, 0.0))) for d, lm in zip(dinvs, lmats)]
            yield
            dinvs = [d - _dot(p.astype(BF16), _blockdiag(d)) for d, p in zip(dinvs, ps)]
            yield
            blk *= 2
        ops = []
        for (a, b), d, qk, ktt, cd in zip(rng, dinvs, qks, ktts, cds):
            tb = d.astype(BF16)
            u = _dot(tb, _blockdiag(vb[a:b]))
            w = _dot(tb, _blockdiag(kbg[a:b])).astype(BF16)
            ops.append((u, w, qd[a:b], qk, ktt, cd))
        return ops

    def recur(ops, r_base):
        for c, (u, w, qd, qk, ktt, cd) in enumerate(ops):
            wq = _dot(jnp.concatenate([w, qd], axis=0), st_s[...].astype(BF16))
            yield
            v_new = u - wq[:CHUNK]
            o_s[pl.ds(r_base + c * CHUNK, CHUNK), :] = wq[CHUNK:] + _dot(qk, _blockdiag(v_new))
            vpad = jnp.concatenate([v_new.astype(BF16), jnp.zeros((LANES - CHUNK, GDN_W), BF16)], axis=0)
            upd = _dot(ktt, vpad)
            yield
            st_s[...] = st_s[...] * cd + jnp.where(same_blk, upd, 0.0)
            yield

    def prepare(si):
        arrs = yield from slab_stage(si)
        return (yield from chunk_stage(arrs))

    def slab_body(si, prev_ops):
        r_prev = pl.multiple_of(jnp.maximum(si - 1, 0) * SLAB, SLAB)
        return _weave(recur(prev_ops, r_prev), prepare(si))[1]

    idle = [(jnp.zeros((CHUNK, GDN_W), F32), jnp.zeros((CHUNK, GDN_W), BF16), jnp.zeros((CHUNK, GDN_W), BF16),
             jnp.zeros((CHUNK, GDN_W), BF16), jnp.zeros((GDN_W, LANES), BF16), jnp.ones((1, GDN_W), F32))
            for _ in range(per_slab)]
    st_s[...] = jnp.zeros_like(st_s)
    last_ops = lax.fori_loop(0, n_slab, slab_body, idle)
    for _ in recur(last_ops, (n_slab - 1) * SLAB):
        pass

    mean_bd = jnp.where(same_blk, 1.0 / GDN_DK, 0.0).astype(BF16)

    def out_body(si, carry):
        r0 = pl.multiple_of(si * SLAB, SLAB)
        o = o_s[pl.ds(r0, SLAB), :]
        oo = o * o
        hi = oo.astype(BF16)
        lo = (oo - hi.astype(F32)).astype(BF16)
        ms = _dot(hi, mean_bd) + _dot(lo, mean_bd)
        y = o * lax.rsqrt(ms + EPS) * ng_ref[...]
        z = qkvz_ref[0, pl.ds(r0, SLAB), 3 * GDN_W:4 * GDN_W]
        o_ref[0, pl.ds(r0, SLAB), :] = (y * _silu(z)).astype(o_ref.dtype)
        return carry

    lax.fori_loop(0, n_slab, out_body, 0)


def _gdn_mixer(proj, conv_w, a_log, dt_bias, norm_g):
    bsz, seq, _ = proj.shape
    nalog = jnp.repeat(-jnp.exp(a_log.astype(F32)), GDN_DK).reshape(1, GDN_W)
    dtb = jnp.repeat(dt_bias.astype(F32), GDN_DK).reshape(1, GDN_W)
    ng = jnp.tile(norm_g.astype(F32), GDN_HEADS).reshape(1, GDN_W)
    row = lambda shape: pl.BlockSpec(shape, lambda b: (0,) * len(shape))
    return pl.pallas_call(
        _gdn_kernel,
        grid=(bsz,),
        in_specs=[pl.BlockSpec((1, seq, 4 * GDN_W), lambda b: (b, 0, COL_GDN // (4 * GDN_W))),
                  pl.BlockSpec((1, seq, LANES), lambda b: (b, 0, COL_MISC // LANES)),
                  row((GDN_CONV, 3 * GDN_W)), row((1, GDN_W)), row((1, GDN_W)), row((1, GDN_W))],
        out_specs=pl.BlockSpec((1, seq, GDN_W), lambda b: (b, 0, 0)),
        out_shape=jax.ShapeDtypeStruct((bsz, seq, GDN_W), BF16),
        scratch_shapes=[pltpu.VMEM((seq, GDN_W), F32),
                        pltpu.VMEM((GDN_W, GDN_W), F32)],
        compiler_params=_cparams(1),
        name="gdn_mixer",
    )(proj, proj, conv_w, nalog, dtb, ng)


def _rg_weights(w_a, w_x):
    def pair_bd(w):
        z = jnp.zeros((RG_BLOCK, RG_BLOCK), w.dtype)
        out = []
        for gi in range(RG_WIDTH // LANES):
            top = jnp.concatenate([w[2 * gi], z], axis=1)
            bot = jnp.concatenate([z, w[2 * gi + 1]], axis=1)
            out.append(jnp.concatenate([top, bot], axis=0))
        return jnp.stack(out)
    return jnp.concatenate([pair_bd(w_a), pair_bd(w_x)], axis=2).astype(BF16)


MLA_TQ = SLAB
PE0 = MLA_NOPE
PE_HALF = MLA_ROPE // 2


def _rms(x, g):
    ms = jnp.mean(x * x, axis=-1, keepdims=True)
    return x * lax.rsqrt(ms + EPS) * g


def _mla_kernel(mq_ref, mkv_ref, misc_ref, pos_ref, invf_ref, qg_ref, kvg_ref, wqt_ref, wk_ref, wvt_ref,
                o_ref, qt_s, k_s, vt_s):
    seq = mq_ref.shape[1]
    n_slab = seq // SLAB
    n_qblk = seq // MLA_TQ
    scale = (MLA_NOPE + MLA_ROPE) ** -0.5
    x1_0, x2_0, pe_end = PE0, PE0 + PE_HALF, PE0 + MLA_ROPE

    def prep_body(si, carry):
        r0 = pl.multiple_of(si * SLAB, SLAB)
        ang_t = invf_ref[...] * pos_ref[0, si].astype(F32)
        cos_t = jnp.cos(ang_t)
        sin_t = jnp.sin(ang_t)

        def rope_t(x1, x2):
            return x1 * cos_t - x2 * sin_t, x2 * cos_t + x1 * sin_t

        qn = _rms(mq_ref[0, pl.ds(r0, SLAB), :], qg_ref[...]).astype(BF16)
        q_t = _dot_nt(wqt_ref[...], qn)
        for hh in range(MLA_HEADS):
            blk = q_t[hh * HEAD_PAD:(hh + 1) * HEAD_PAD]
            x1, x2 = rope_t(blk[x1_0:x2_0], blk[x2_0:pe_end])
            roped = jnp.concatenate([blk[:PE0], x1, x2, blk[pe_end:]], axis=0)
            qt_s[si, hh * HEAD_PAD:(hh + 1) * HEAD_PAD, :] = (roped * scale).astype(BF16)
        kvn = _rms(mkv_ref[0, pl.ds(r0, SLAB), :], kvg_ref[...]).astype(BF16)
        kn = _dot(kvn, wk_ref[...])
        vt_s[si] = _dot_nt(wvt_ref[...], kvn).astype(BF16)
        misc_t = misc_ref[0, pl.ds(r0, SLAB), :].T
        k1, k2 = rope_t(misc_t[:PE_HALF], misc_t[PE_HALF:MLA_ROPE])
        kr = jnp.concatenate([jnp.zeros((PE0, SLAB), F32), k1, k2,
                              jnp.zeros((HEAD_PAD - pe_end, SLAB), F32)], axis=0).T
        for hh in range(MLA_HEADS):
            l0, l1 = hh * HEAD_PAD, (hh + 1) * HEAD_PAD
            k_s[pl.ds(r0, SLAB), l0:l1] = (kn[:, l0:l1] + kr).astype(BF16)
        return carry

    lax.fori_loop(0, n_slab, prep_body, 0)

    ki = lax.broadcasted_iota(jnp.int32, (MLA_TQ, MLA_TQ), 0)
    qi_ = lax.broadcasted_iota(jnp.int32, (MLA_TQ, MLA_TQ), 1)
    diag_mask = (ki // CHUNK) <= (qi_ // CHUNK)

    def q_body(qi, carry):
        q0 = pl.multiple_of(qi * MLA_TQ, MLA_TQ)

        def score(kj):
            k0 = pl.multiple_of(kj * MLA_TQ, MLA_TQ)
            return tuple(_dot(k_s[pl.ds(k0, MLA_TQ), hh * HEAD_PAD:(hh + 1) * HEAD_PAD],
                              qt_s[qi, hh * HEAD_PAD:(hh + 1) * HEAD_PAD, :])
                         for hh in range(MLA_HEADS))

        def attend(kj, scores, state, masked):
            stats, probs = [], []
            for hh, s in enumerate(scores):
                m, l, _ = state[hh]
                if masked:
                    s = jnp.where(diag_mask, s, -jnp.inf)
                m_new = jnp.maximum(m, jnp.max(s, axis=0, keepdims=True))
                alpha = jnp.exp(m - m_new)
                p = jnp.exp(s - m_new)
                stats.append((m_new, alpha * l + jnp.sum(p, axis=0, keepdims=True), alpha))
                probs.append(p.astype(BF16))
            return tuple(
                (m_new, l, alpha * state[hh][2] + _dot(vt_s[kj, hh * MLA_V:(hh + 1) * MLA_V, :], probs[hh]))
                for hh, (m_new, l, alpha) in enumerate(stats))

        init = tuple((jnp.full((1, MLA_TQ), -jnp.inf, F32), jnp.zeros((1, MLA_TQ), F32),
                      jnp.zeros((MLA_V, MLA_TQ), F32)) for _ in range(MLA_HEADS))
        state = attend(qi, score(qi), init, True)
        state = lax.fori_loop(0, qi, lambda kj, st: attend(kj, score(kj), st, False), state)
        out_t = jnp.concatenate([acc / l for (_, l, acc) in state], axis=0)
        o_ref[0, pl.ds(q0, MLA_TQ), :] = out_t.T.astype(o_ref.dtype)
        return carry

    lax.fori_loop(0, n_qblk, q_body, 0)


def _mla_weights(w_qb, w_kvb):
    qr = w_qb.reshape(MLA_Q_RANK, MLA_HEADS, MLA_NOPE + MLA_ROPE)
    zq = jnp.zeros((MLA_Q_RANK, MLA_HEADS, HEAD_PAD - MLA_NOPE - MLA_ROPE), w_qb.dtype)
    wq = jnp.concatenate([qr, zq], axis=2).reshape(MLA_Q_RANK, MLA_HEADS * HEAD_PAD)
    kvr = w_kvb.reshape(MLA_KV_RANK, MLA_HEADS, MLA_NOPE + MLA_V)
    zk = jnp.zeros((MLA_KV_RANK, MLA_HEADS, HEAD_PAD - MLA_NOPE), w_kvb.dtype)
    wk = jnp.concatenate([kvr[:, :, :MLA_NOPE], zk], axis=2).reshape(MLA_KV_RANK, MLA_HEADS * HEAD_PAD)
    wvt = kvr[:, :, MLA_NOPE:].reshape(MLA_KV_RANK, MLA_HEADS * MLA_V).T
    return wq.T.astype(BF16), wk.astype(BF16), wvt.astype(BF16)


def _mla_mixer(proj, positions, q_norm_g, w_qb, kv_norm_g, w_kvb):
    bsz, seq, _ = proj.shape
    inv_freq = ROPE_THETA ** (-jnp.arange(0, MLA_ROPE, 2, dtype=F32) / MLA_ROPE)
    invf = inv_freq.reshape(PE_HALF, 1)
    wqt, wk, wvt = _mla_weights(w_qb, w_kvb)
    wide = MLA_HEADS * HEAD_PAD
    vw = MLA_HEADS * MLA_V
    row = lambda shape: pl.BlockSpec(shape, lambda b: (0,) * len(shape))
    return pl.pallas_call(
        _mla_kernel,
        grid=(bsz,),
        in_specs=[pl.BlockSpec((1, seq, MLA_Q_RANK), lambda b: (b, 0, COL_MQ // MLA_Q_RANK)),
                  pl.BlockSpec((1, seq, MLA_KV_RANK), lambda b: (b, 0, COL_MKV // MLA_KV_RANK)),
                  pl.BlockSpec((1, seq, LANES), lambda b: (b, 0, COL_MISC // LANES)),
                  pl.BlockSpec((1, seq // SLAB, 1, SLAB), lambda b: (b, 0, 0, 0)),
                  row((PE_HALF, 1)), row((1, MLA_Q_RANK)), row((1, MLA_KV_RANK)),
                  row((wide, MLA_Q_RANK)), row((MLA_KV_RANK, wide)), row((vw, MLA_KV_RANK))],
        out_specs=pl.BlockSpec((1, seq, vw), lambda b: (b, 0, 0)),
        out_shape=jax.ShapeDtypeStruct((bsz, seq, vw), BF16),
        scratch_shapes=[pltpu.VMEM((seq // SLAB, wide, SLAB), BF16), pltpu.VMEM((seq, wide), BF16),
                        pltpu.VMEM((seq // SLAB, vw, SLAB), BF16)],
        compiler_params=_cparams(1),
        name="mla_mixer",
    )(proj, proj, proj, positions.reshape(bsz, seq // SLAB, 1, SLAB), invf,
      q_norm_g.astype(F32).reshape(1, MLA_Q_RANK), kv_norm_g.astype(F32).reshape(1, MLA_KV_RANK), wqt, wk, wvt)


FF_TILE = 1024


def _order_after(x, anchor):
    zero_bits = lax.shift_right_logical(pltpu.bitcast(anchor, jnp.uint32), jnp.uint32(32))
    return x + pltpu.bitcast(zero_bits, F32)


def _rg_front(rx, tail, convw_ref, convb_ref, wrg_ref, xc_s, pre_s):
    convw = convw_ref[...]
    for s0 in range(0, rx.shape[0], SLAB):
        prev = tail if s0 == 0 else rx[s0 - SUBLANES:s0]
        xc = _conv_taps(prev, rx[s0:s0 + SLAB], convw) + convb_ref[...]
        xc_s[s0:s0 + SLAB, :] = xc
        for gi, pre in enumerate(_rg_gates(xc, wrg_ref)):
            pre_s[s0:s0 + SLAB, gi * 2 * LANES:(gi + 1) * 2 * LANES] = pre


def _rg_back(xc_s, pre_s, gate, anchor, ba_ref, bx_ref, lam_ref, hc_s, ob_s):
    for s0 in range(0, xc_s.shape[0], SLAB):
        pres = [pre_s[s0:s0 + SLAB, gi * 2 * LANES:(gi + 1) * 2 * LANES] for gi in range(RG_WIDTH // LANES)]
        if anchor is not None:
            pres = [_order_after(p, anchor) for p in pres]
        outs = _rg_scan(xc_s[s0:s0 + SLAB, :], pres, gate[s0:s0 + SLAB], ba_ref, bx_ref, lam_ref, hc_s)
        for gi, o in enumerate(outs):
            ob_s[s0:s0 + SLAB, gi * LANES:(gi + 1) * LANES] = o.astype(ob_s.dtype)


def _outmlp_kernel(h_ref, oa_ref, oc_ref, gtm_ref, shf_ref, scf_ref, gtf_ref, g_ref, fg_ref,
                   wo_ref, w1_ref, w2_ref,
                   rx0_ref, gate0_ref, rxn_ref, gaten_ref, tailn_ref,
                   convw_ref, convb_ref, wrg_ref, ba_ref, bx_ref, lam_ref,
                   o_ref, ob_s, hc_s, xc_s, pre_s, *, final_norm):
    n_i = pl.num_programs(1)
    step = pl.program_id(0) * n_i + pl.program_id(1)
    front_refs = (convw_ref, convb_ref, wrg_ref, xc_s, pre_s)
    back_refs = (ba_ref, bx_ref, lam_ref, hc_s, ob_s)

    @pl.when(step == 0)
    def _():
        hc_s[...] = jnp.zeros_like(hc_s)
        _rg_front(rx0_ref[0], jnp.zeros((SUBLANES, RG_WIDTH), F32), *front_refs)
        _rg_back(xc_s, pre_s, gate0_ref[0], None, *back_refs)

    seq_head = (step + 1) % n_i == 0
    hc_s[...] = jnp.where(seq_head, 0.0, hc_s[...])
    _rg_front(rxn_ref[0], jnp.where(seq_head, 0.0, tailn_ref[0]), *front_refs)
    na, nb = oa_ref.shape[2], ob_s.shape[1]
    mix = (_dot(oa_ref[0], wo_ref[0:na, :]) + _dot(ob_s[...], wo_ref[na:na + nb, :])
           + _dot(oc_ref[0], wo_ref[na + nb:, :]))
    h1 = h_ref[0] + gtm_ref[0] * mix
    u = _modnorm(h1, g_ref[...], scf_ref[0], shf_ref[0]).astype(BF16)
    d_ff = w1_ref.shape[1]
    f = jnp.zeros_like(h1)
    anchor = None
    for t in range(d_ff // FF_TILE):
        hid = jnp.maximum(_dot(u, w1_ref[:, t * FF_TILE:(t + 1) * FF_TILE]), 0.0)
        if t == 0:
            anchor = hid[0:1, 0:2 * LANES]
        f = f + _dot((hid * hid).astype(BF16), w2_ref[t * FF_TILE:(t + 1) * FF_TILE, :])
    h2 = h1 + gtf_ref[0] * f
    if final_norm:
        ms = jnp.mean(h2 * h2, axis=-1, keepdims=True)
        h2 = h2 * lax.rsqrt(ms + EPS) * fg_ref[...]
    o_ref[0] = h2
    _rg_back(xc_s, pre_s, gaten_ref[0], anchor, *back_refs)


def _out_mlp(h, o_a, o_c, proj, mod_l, g, final_g, w_out, w1, w2, rg_params, final_norm, tm=512):
    bsz, seq, d = h.shape
    d_ff = w1.shape[1]
    n_i = seq // tm
    last = bsz * n_i - 1
    mod3 = mod_l.reshape(bsz, 1, N_MOD * d)
    conv_w, conv_b, w_a, b_a, w_x, b_x, lam = rg_params
    r2 = lambda v: v.astype(F32).reshape(1, RG_WIDTH)
    modspec = lambda m: pl.BlockSpec((1, 1, d), lambda b, i: (b, 0, m))
    tile = lambda n: pl.BlockSpec((1, tm, n), lambda b, i: (b, i, 0))
    const = lambda shape: pl.BlockSpec(shape, lambda b, i: (0,) * len(shape), pipeline_mode=pl.Buffered(1))

    def nxt(b, i):
        lin = jnp.minimum(b * n_i + i + 1, last)
        return lin // n_i, lin % n_i

    def next_tile(col):
        return pl.BlockSpec((1, tm, RG_WIDTH), lambda b, i: (*nxt(b, i), col // RG_WIDTH))

    def next_tail(b, i):
        nb, ni = nxt(b, i)
        return nb, jnp.maximum(ni * (tm // SUBLANES) - 1, 0), COL_RX // RG_WIDTH

    first_tile = lambda col: pl.BlockSpec((1, tm, RG_WIDTH), lambda b, i: (0, 0, col // RG_WIDTH))
    return pl.pallas_call(
        functools.partial(_outmlp_kernel, final_norm=final_norm),
        grid=(bsz, n_i),
        in_specs=[tile(d), tile(o_a.shape[2]), tile(o_c.shape[2]),
                  modspec(2), modspec(3), modspec(4), modspec(5),
                  const((1, d)), const((1, d)),
                  const((d, d)), const((d, d_ff)), const((d_ff, d)),
                  first_tile(COL_RX), first_tile(COL_RGATE), next_tile(COL_RX), next_tile(COL_RGATE),
                  pl.BlockSpec((1, SUBLANES, RG_WIDTH), next_tail),
                  const((RG_CONV, RG_WIDTH)), const((1, RG_WIDTH)),
                  const((RG_WIDTH // LANES, LANES, 2 * LANES)),
                  const((1, RG_WIDTH)), const((1, RG_WIDTH)), const((1, RG_WIDTH))],
        out_specs=tile(d),
        out_shape=jax.ShapeDtypeStruct((bsz, seq, d), F32),
        scratch_shapes=[pltpu.VMEM((tm, RG_WIDTH), BF16),
                        pltpu.VMEM((SUBLANES, RG_WIDTH), F32),
                        pltpu.VMEM((tm, RG_WIDTH), F32),
                        pltpu.VMEM((tm, 2 * RG_WIDTH), F32)],
        compiler_params=_cparams(2),
        name="outproj_mlp_rglru",
    )(h, o_a, o_c, mod3, mod3, mod3, mod3, g.reshape(1, d), final_g.reshape(1, d), w_out, w1, w2,
      proj, proj, proj, proj, proj,
      conv_w, r2(conv_b), _rg_weights(w_a, w_x), r2(b_a), r2(b_x), r2(lam))


def _pad_w_in(w_in_l):
    d = w_in_l.shape[0]
    o = 0
    parts = {}
    for name, n in (("gqkvz", 4 * GDN_W), ("ga", GDN_HEADS), ("gb", GDN_HEADS), ("rx", RG_WIDTH),
                    ("rgate", RG_WIDTH), ("mq", MLA_Q_RANK), ("mkv", MLA_KV_RANK), ("mkr", MLA_ROPE)):
        parts[name] = w_in_l[:, o:o + n]
        o += n
    pad = jnp.zeros((d, NP_IN - COL_MISC - MLA_ROPE - 2 * GDN_HEADS), w_in_l.dtype)
    return jnp.concatenate([parts["gqkvz"], parts["rx"], parts["rgate"], parts["mq"], parts["mkv"],
                            parts["mkr"], parts["ga"], parts["gb"], pad], axis=1).astype(BF16)


def kernel(x, c, positions, w_mod, b_mod, norm_mix_g, w_in, gdn_conv_w, gdn_a_log, gdn_dt_bias, gdn_norm_g, rg_conv_w, rg_conv_b, rg_w_a, rg_b_a, rg_w_x, rg_b_x, rg_lambda, mla_q_norm_g, mla_w_qb, mla_kv_norm_g, mla_w_kvb, w_out, norm_mlp_g, w_mlp_in, w_mlp_out, final_norm_g):
    depth = w_mod.shape[0]
    mod = _modulation(c, w_mod, b_mod)
    h = x
    for l in range(depth):
        rg_params = (rg_conv_w[l], rg_conv_b[l], rg_w_a[l], rg_b_a[l], rg_w_x[l], rg_b_x[l], rg_lambda[l])
        proj = _in_projection(h, mod[l], norm_mix_g[l], _pad_w_in(w_in[l]))
        o_a = _gdn_mixer(proj, gdn_conv_w[l], gdn_a_log[l], gdn_dt_bias[l], gdn_norm_g[l])
        o_c = _mla_mixer(proj, positions, mla_q_norm_g[l], mla_w_qb[l], mla_kv_norm_g[l], mla_w_kvb[l])
        h = _out_mlp(h, o_a, o_c, proj, mod[l], norm_mlp_g[l], final_norm_g,
                     w_out[l].astype(BF16), w_mlp_in[l].astype(BF16), w_mlp_out[l].astype(BF16),
                     rg_params, final_norm=(l == depth - 1))
    return h
```

```python
import functools

import jax
import jax.numpy as jnp
from jax import lax
from jax.experimental import pallas as pl
from jax.experimental.pallas import tpu as pltpu

F32 = jnp.float32
BF16 = jnp.bfloat16

EPS = 1e-6
N_MOD = 6
CHUNK = 64
GDN_HEADS = 4
GDN_DK = 64
GDN_W = GDN_HEADS * GDN_DK
GDN_CONV = 4
RG_WIDTH = 512
RG_BLOCK = 64
RG_CONV = 4
RG_C = 8.0
MLA_HEADS = 4
MLA_NOPE = 64
MLA_ROPE = 32
MLA_V = 64
MLA_Q_RANK = 256
MLA_KV_RANK = 128
ROPE_THETA = 10000.0
LANES = 128
SUBLANES = 8
HEAD_PAD = LANES

COL_GDN = 0
COL_RX = 1024
COL_RGATE = 1536
COL_MQ = 2048
COL_MKV = 2304
COL_MISC = 2432
NP_IN = 2560
MISC_A = MLA_ROPE
MISC_B = MLA_ROPE + GDN_HEADS

SLAB = 256
VMEM_LIMIT = 56 * 1024 * 1024


def _cparams(n_axes):
    return pltpu.CompilerParams(dimension_semantics=("arbitrary",) * n_axes,
                                vmem_limit_bytes=VMEM_LIMIT)


def _dot(a, b):
    return jnp.dot(a, b, preferred_element_type=F32)


def _dot_nt(a, b):
    return lax.dot_general(a, b, (((1,), (1,)), ((), ())), preferred_element_type=F32)


def _split3(x):
    hi = x.astype(BF16)
    r1 = x - hi.astype(F32)
    mid = r1.astype(BF16)
    lo = (r1 - mid.astype(F32)).astype(BF16)
    return hi, mid, lo


def _dot3_lhs(x, w):
    hi, mid, lo = _split3(x)
    return _dot(hi, w) + _dot(mid, w) + _dot(lo, w)


def _dot3_rhs(w, x):
    hi, mid, lo = _split3(x)
    return _dot(w, hi) + _dot(w, mid) + _dot(w, lo)


def _sigmoid(x):
    return 1.0 / (1.0 + jnp.exp(-x))


def _silu(x):
    return x * _sigmoid(x)


def _softplus(x):
    return jnp.maximum(x, 0.0) + jnp.log1p(jnp.exp(-jnp.abs(x)))


def _conv_taps(prev, cur, w):
    taps = w.shape[0]
    xcat = jnp.concatenate([prev, cur], axis=0)
    acc = cur * w[taps - 1:taps, :]
    for j in range(taps - 1):
        shifted = pltpu.roll(xcat, taps - 1 - j, axis=0)[SUBLANES:, :]
        acc = acc + shifted * w[j:j + 1, :]
    return acc


def _causal_conv(ref, r0, rows, col0, ncols, w):
    cur = ref[0, pl.ds(r0, rows), col0:col0 + ncols]
    pstart = pl.multiple_of(jnp.maximum(r0 - SUBLANES, 0), SUBLANES)
    prev = ref[0, pl.ds(pstart, SUBLANES), col0:col0 + ncols]
    return _conv_taps(jnp.where(r0 > 0, prev, 0.0), cur, w)


def _mod_kernel(c_ref, w_ref, b_ref, o_ref):
    ca = _silu(c_ref[...])
    o_ref[0] = _dot(ca.astype(BF16), w_ref[0].astype(BF16)) + b_ref[0]


def _modulation(c, w_mod, b_mod):
    depth, d, n = w_mod.shape
    bsz = c.shape[0]
    tn = 1536
    return pl.pallas_call(
        _mod_kernel,
        grid=(depth, n // tn),
        in_specs=[pl.BlockSpec((bsz, d), lambda l, j: (0, 0)),
                  pl.BlockSpec((1, d, tn), lambda l, j: (l, 0, j)),
                  pl.BlockSpec((1, 1, tn), lambda l, j: (l, 0, j))],
        out_specs=pl.BlockSpec((1, bsz, tn), lambda l, j: (l, 0, j)),
        out_shape=jax.ShapeDtypeStruct((depth, bsz, n), F32),
        compiler_params=_cparams(2),
        name="adaln_mod",
    )(c, w_mod, b_mod.reshape(depth, 1, n))


def _modnorm(x, g, sc, sh):
    ms = jnp.mean(x * x, axis=-1, keepdims=True)
    return (x * lax.rsqrt(ms + EPS) * g) * (1.0 + sc) + sh


def _gelu_tanh(x):
    return 0.5 * x * (1.0 + jnp.tanh(0.7978845608028654 * (x + 0.044715 * (x * x * x))))


def _rg_gates(xc, w_ref):
    return [_dot(xc[:, gi * LANES:(gi + 1) * LANES].astype(BF16), w_ref[gi]) for gi in range(RG_WIDTH // LANES)]


def _rg_scan(xc_all, pres, gate, ba_ref, bx_ref, lam_ref, hc_s):
    n_tile = SLAB // SUBLANES
    sub = lax.broadcasted_iota(jnp.int32, (n_tile, SUBLANES, LANES), 1)
    outs = []
    for gi, pre in enumerate(pres):
        l0, l1 = gi * LANES, (gi + 1) * LANES
        if outs:
            done = outs[-1][0:1, :]
            pre = _order_after(pre, jnp.concatenate([done, done], axis=1))
        xc = xc_all[:, l0:l1]
        r = _sigmoid(pre[:, :LANES] + ba_ref[:, l0:l1])
        ig = _sigmoid(pre[:, LANES:] + bx_ref[:, l0:l1])
        log_a = (-RG_C) * r * _softplus(-lam_ref[:, l0:l1])
        a = jnp.exp(log_a)
        om = 1.0 - a * a
        mult = jnp.where(om > 0.0, om * lax.rsqrt(om), 0.0)
        bt = mult * (ig * xc)
        a3 = a.reshape(n_tile, SUBLANES, LANES)
        b3 = bt.reshape(n_tile, SUBLANES, LANES)
        d = 1
        while d < SUBLANES:
            a_sh = pltpu.roll(a3, d, axis=1)
            b_sh = pltpu.roll(b3, d, axis=1)
            keep = sub >= d
            b3 = jnp.where(keep, a3 * b_sh + b3, b3)
            a3 = jnp.where(keep, a3 * a_sh, a3)
            d *= 2
        car = hc_s[:, l0:l1]
        hs = []
        for n in range(n_tile):
            hn = a3[n] * car + b3[n]
            hs.append(hn)
            car = jnp.broadcast_to(hn[SUBLANES - 1:SUBLANES, :], (SUBLANES, LANES))
        hc_s[:, l0:l1] = car
        outs.append(jnp.concatenate(hs, axis=0) * _gelu_tanh(gate[:, l0:l1]))
    return outs


def _inproj_kernel(h_ref, sh_ref, sc_ref, g_ref, w_ref, o_ref):
    u = _modnorm(h_ref[0], g_ref[...], sc_ref[0], sh_ref[0])
    o_ref[0] = _dot(u.astype(BF16), w_ref[...])


def _in_projection(h, mod_l, g, w_pad, tm=512):
    bsz, seq, d = h.shape
    npad = w_pad.shape[1]
    mod3 = mod_l.reshape(bsz, 1, N_MOD * d)
    return pl.pallas_call(
        _inproj_kernel,
        grid=(bsz, seq // tm),
        in_specs=[pl.BlockSpec((1, tm, d), lambda b, i: (b, i, 0)),
                  pl.BlockSpec((1, 1, d), lambda b, i: (b, 0, 0)),
                  pl.BlockSpec((1, 1, d), lambda b, i: (b, 0, 1)),
                  pl.BlockSpec((1, d), lambda b, i: (0, 0)),
                  pl.BlockSpec((d, npad), lambda b, i: (0, 0))],
        out_specs=pl.BlockSpec((1, tm, npad), lambda b, i: (b, i, 0)),
        out_shape=jax.ShapeDtypeStruct((bsz, seq, npad), F32),
        compiler_params=_cparams(2),
        name="norm_inproj",
    )(h, mod3, mod3, g.reshape(1, d), w_pad)


def _head_of(idx):
    return idx // GDN_DK


def _blockdiag(x):
    xb = x.astype(BF16)
    lane_head = _head_of(lax.broadcasted_iota(jnp.int32, xb.shape, 1))
    zero = jnp.zeros_like(xb)
    return jnp.concatenate([jnp.where(lane_head == hh, xb, zero) for hh in range(GDN_HEADS)], axis=0)


def _weave(*stages, start=None):
    start = start or [0] * len(stages)
    results = [None] * len(stages)
    live = list(range(len(stages)))
    turn = 0
    while live:
        for idx in list(live):
            if turn < start[idx]:
                continue
            try:
                next(stages[idx])
            except StopIteration as stop:
                results[idx] = stop.value
                live.remove(idx)
        turn += 1
    return results


def _gdn_kernel(qkvz_ref, misc_ref, convw_ref, nalog_ref, dtb_ref, ng_ref, o_ref, o_s, st_s):
    seq = qkvz_ref.shape[1]
    n_slab = seq // SLAB
    per_slab = SLAB // CHUNK

    ii = lax.broadcasted_iota(jnp.int32, (GDN_W, GDN_W), 0)
    jj = lax.broadcasted_iota(jnp.int32, (GDN_W, GDN_W), 1)
    same_blk = _head_of(ii) == _head_of(jj)
    ones_bd = jnp.where(same_blk, 1.0, 0.0).astype(BF16)
    tri_bd = jnp.where(same_blk & (ii >= jj), 1.0, 0.0).astype(BF16)
    mi = lax.broadcasted_iota(jnp.int32, (LANES, GDN_W), 0)
    mj = lax.broadcasted_iota(jnp.int32, (LANES, GDN_W), 1)
    sel_a = jnp.where(mi == MISC_A + _head_of(mj), 1.0, 0.0).astype(BF16)
    sel_b = jnp.where(mi == MISC_B + _head_of(mj), 1.0, 0.0).astype(BF16)
    ci = lax.broadcasted_iota(jnp.int32, (CHUNK, GDN_W), 0)
    cj = lax.broadcasted_iota(jnp.int32, (CHUNK, GDN_W), 1)
    cs = cj % CHUNK
    lower_incl = ci >= cs
    strict_lower = ci > cs
    eye_hl = jnp.where(ci == cs, 1.0, 0.0)
    convw = convw_ref[...]
    rng = [(c * CHUNK, (c + 1) * CHUNK) for c in range(per_slab)]

    def slab_stage(si):
        r0 = pl.multiple_of(si * SLAB, SLAB)
        y = _silu(_causal_conv(qkvz_ref, r0, SLAB, 0, 3 * GDN_W, convw))
        q, k, v = y[:, :GDN_W], y[:, GDN_W:2 * GDN_W], y[:, 2 * GDN_W:]

        def l2n(x):
            xx = x * x
            hi = xx.astype(BF16)
            lo = (xx - hi.astype(F32)).astype(BF16)
            ss = _dot(hi, ones_bd) + _dot(lo, ones_bd)
            return x * lax.rsqrt(ss + EPS)

        q = l2n(q) * (GDN_DK ** -0.5)
        k = l2n(k)
        yield
        misc = misc_ref[0, pl.ds(r0, SLAB), :]
        a_exp = _dot3_lhs(misc, sel_a)
        b_exp = _dot3_lhs(misc, sel_b)
        g = nalog_ref[...] * _softplus(a_exp + dtb_ref[...])
        beta = _sigmoid(b_exp)
        yield
        gcum = _dot3_rhs(tri_bd, g)
        glast = _dot3_rhs(ones_bd, g)
        eg = jnp.exp(gcum)
        kb = k * beta
        vb = (v * beta).astype(BF16)
        kbg = (kb * eg).astype(BF16)
        qd = (q * eg).astype(BF16)
        kt = k * jnp.exp(glast - gcum)
        cdec = jnp.exp(glast)
        ktts = []
        for a, b in rng:
            ktc = jnp.concatenate([kt[a:b], jnp.zeros((LANES - CHUNK, GDN_W), F32)], axis=0)
            ktts.append(ktc.T.astype(BF16))
        yield
        return (q.astype(BF16), k.astype(BF16), kb.astype(BF16), vb, kbg, qd, gcum, ktts,
                [cdec[a:a + 1] for a, _ in rng])

    def chunk_stage(arrs):
        q, k, kb, vb, kbg, qd, gcum, ktts, cds = arrs
        aqs = [_dot_nt(jnp.concatenate([kb[a:b], q[a:b]], axis=0), _blockdiag(k[a:b])) for a, b in rng]
        yield

        g_t = [gcum[:, :LANES].T, gcum[:, LANES:].T]
        head_rows = [g_t[hh // 2][(hh % 2) * GDN_DK:(hh % 2) * GDN_DK + 1, :] for hh in range(GDN_HEADS)]
        grs = [jnp.concatenate([row[:, a:b] for row in head_rows], axis=1) for a, b in rng]
        decays = [jnp.exp(jnp.where(lower_incl, gcum[a:b] - gr, 0.0)) for (a, b), gr in zip(rng, grs)]
        yield
        lmats = [jnp.where(strict_lower, aq[:CHUNK] * dec, 0.0) for aq, dec in zip(aqs, decays)]
        qks = [jnp.where(lower_incl, aq[CHUNK:] * dec, 0.0).astype(BF16) for aq, dec in zip(aqs, decays)]
        dinvs = [eye_hl - jnp.where(ci // 2 == cs // 2, lm, 0.0) for lm in lmats]
        blk = 2
        while blk < CHUNK:
            off_diag = (ci // (2 * blk) == cs // (2 * blk)) & (ci // blk != cs // blk)
            ps = [_dot(d.astype(BF16), _blockdiag(jnp.where(off_diag, lm, 0.0))) for d, lm in zip(dinvs, lmats)]
            yield
            dinvs = [d - _dot(p.astype(BF16), _blockdiag(d)) for d, p in zip(dinvs, ps)]
            yield
            blk *= 2
        ops = []
        for (a, b), d, qk, ktt, cd in zip(rng, dinvs, qks, ktts, cds):
            tb = d.astype(BF16)
            u = _dot(tb, _blockdiag(vb[a:b]))
            w = _dot(tb, _blockdiag(kbg[a:b])).astype(BF16)
            ops.append((u, w, qd[a:b], qk, ktt, cd))
        return ops

    def recur(ops, r_base):
        for c, (u, w, qd, qk, ktt, cd) in enumerate(ops):
            wq = _dot(jnp.concatenate([w, qd], axis=0), st_s[...].astype(BF16))
            yield
            v_new = u - wq[:CHUNK]
            o_s[pl.ds(r_base + c * CHUNK, CHUNK), :] = wq[CHUNK:] + _dot(qk, _blockdiag(v_new))
            vpad = jnp.concatenate([v_new.astype(BF16), jnp.zeros((LANES - CHUNK, GDN_W), BF16)], axis=0)
            upd = _dot(ktt, vpad)
            yield
            st_s[...] = st_s[...] * cd + jnp.where(same_blk, upd, 0.0)
            yield

    def prepare(si):
        arrs = yield from slab_stage(si)
        return (yield from chunk_stage(arrs))

    def slab_body(si, prev_ops):
        r_prev = pl.multiple_of(jnp.maximum(si - 1, 0) * SLAB, SLAB)
        return _weave(recur(prev_ops, r_prev), prepare(si))[1]

    idle = [(jnp.zeros((CHUNK, GDN_W), F32), jnp.zeros((CHUNK, GDN_W), BF16), jnp.zeros((CHUNK, GDN_W), BF16),
             jnp.zeros((CHUNK, GDN_W), BF16), jnp.zeros((GDN_W, LANES), BF16), jnp.ones((1, GDN_W), F32))
            for _ in range(per_slab)]
    st_s[...] = jnp.zeros_like(st_s)
    last_ops = lax.fori_loop(0, n_slab, slab_body, idle)

    mean_bd = jnp.where(same_blk, 1.0 / GDN_DK, 0.0).astype(BF16)

    def gated_norm(r0):
        o = o_s[pl.ds(r0, SLAB), :]
        oo = o * o
        hi = oo.astype(BF16)
        lo = (oo - hi.astype(F32)).astype(BF16)
        ms = _dot(hi, mean_bd) + _dot(lo, mean_bd)
        y = o * lax.rsqrt(ms + EPS) * ng_ref[...]
        z = qkvz_ref[0, pl.ds(r0, SLAB), 3 * GDN_W:4 * GDN_W]
        o_ref[0, pl.ds(r0, SLAB), :] = (y * _silu(z)).astype(o_ref.dtype)

    def finished_slabs():
        for si in range(n_slab - 1):
            gated_norm(si * SLAB)
            yield

    _weave(recur(last_ops, (n_slab - 1) * SLAB), finished_slabs())
    gated_norm((n_slab - 1) * SLAB)


def _gdn_mixer(proj, conv_w, a_log, dt_bias, norm_g):
    bsz, seq, _ = proj.shape
    nalog = jnp.repeat(-jnp.exp(a_log.astype(F32)), GDN_DK).reshape(1, GDN_W)
    dtb = jnp.repeat(dt_bias.astype(F32), GDN_DK).reshape(1, GDN_W)
    ng = jnp.tile(norm_g.astype(F32), GDN_HEADS).reshape(1, GDN_W)
    row = lambda shape: pl.BlockSpec(shape, lambda b: (0,) * len(shape))
    return pl.pallas_call(
        _gdn_kernel,
        grid=(bsz,),
        in_specs=[pl.BlockSpec((1, seq, 4 * GDN_W), lambda b: (b, 0, COL_GDN // (4 * GDN_W))),
                  pl.BlockSpec((1, seq, LANES), lambda b: (b, 0, COL_MISC // LANES)),
                  row((GDN_CONV, 3 * GDN_W)), row((1, GDN_W)), row((1, GDN_W)), row((1, GDN_W))],
        out_specs=pl.BlockSpec((1, seq, GDN_W), lambda b: (b, 0, 0)),
        out_shape=jax.ShapeDtypeStruct((bsz, seq, GDN_W), BF16),
        scratch_shapes=[pltpu.VMEM((seq, GDN_W), F32),
                        pltpu.VMEM((GDN_W, GDN_W), F32)],
        compiler_params=_cparams(1),
        name="gdn_mixer",
    )(proj, proj, conv_w, nalog, dtb, ng)


def _rg_weights(w_a, w_x):
    def pair_bd(w):
        z = jnp.zeros((RG_BLOCK, RG_BLOCK), w.dtype)
        out = []
        for gi in range(RG_WIDTH // LANES):
            top = jnp.concatenate([w[2 * gi], z], axis=1)
            bot = jnp.concatenate([z, w[2 * gi + 1]], axis=1)
            out.append(jnp.concatenate([top, bot], axis=0))
        return jnp.stack(out)
    return jnp.concatenate([pair_bd(w_a), pair_bd(w_x)], axis=2).astype(BF16)


MLA_TQ = SLAB
PE0 = MLA_NOPE
PE_HALF = MLA_ROPE // 2


def _rms(x, g):
    ms = jnp.mean(x * x, axis=-1, keepdims=True)
    return x * lax.rsqrt(ms + EPS) * g


def _mla_kernel(mq_ref, mkv_ref, misc_ref, pos_ref, invf_ref, qg_ref, kvg_ref, wqt_ref, wk_ref, wvt_ref,
                o_ref, qt_s, k_s, vt_s):
    seq = mq_ref.shape[1]
    n_slab = seq // SLAB
    n_qblk = seq // MLA_TQ
    scale = (MLA_NOPE + MLA_ROPE) ** -0.5
    x1_0, x2_0, pe_end = PE0, PE0 + PE_HALF, PE0 + MLA_ROPE

    def prep_body(si, carry):
        r0 = pl.multiple_of(si * SLAB, SLAB)
        ang_t = invf_ref[...] * pos_ref[0, si].astype(F32)
        cos_t = jnp.cos(ang_t)
        sin_t = jnp.sin(ang_t)

        def rope_t(x1, x2):
            return x1 * cos_t - x2 * sin_t, x2 * cos_t + x1 * sin_t

        qn = _rms(mq_ref[0, pl.ds(r0, SLAB), :], qg_ref[...]).astype(BF16)
        q_t = _dot_nt(wqt_ref[...], qn)
        for hh in range(MLA_HEADS):
            blk = q_t[hh * HEAD_PAD:(hh + 1) * HEAD_PAD]
            x1, x2 = rope_t(blk[x1_0:x2_0], blk[x2_0:pe_end])
            roped = jnp.concatenate([blk[:PE0], x1, x2, blk[pe_end:]], axis=0)
            qt_s[si, hh * HEAD_PAD:(hh + 1) * HEAD_PAD, :] = (roped * scale).astype(BF16)
        kvn = _rms(mkv_ref[0, pl.ds(r0, SLAB), :], kvg_ref[...]).astype(BF16)
        kn = _dot(kvn, wk_ref[...])
        vt_s[si] = _dot_nt(wvt_ref[...], kvn).astype(BF16)
        misc_t = misc_ref[0, pl.ds(r0, SLAB), :].T
        k1, k2 = rope_t(misc_t[:PE_HALF], misc_t[PE_HALF:MLA_ROPE])
        kr = jnp.concatenate([jnp.zeros((PE0, SLAB), F32), k1, k2,
                              jnp.zeros((HEAD_PAD - pe_end, SLAB), F32)], axis=0).T
        for hh in range(MLA_HEADS):
            l0, l1 = hh * HEAD_PAD, (hh + 1) * HEAD_PAD
            k_s[pl.ds(r0, SLAB), l0:l1] = (kn[:, l0:l1] + kr).astype(BF16)
        return carry

    lax.fori_loop(0, n_slab, prep_body, 0)

    ki = lax.broadcasted_iota(jnp.int32, (MLA_TQ, MLA_TQ), 0)
    qi_ = lax.broadcasted_iota(jnp.int32, (MLA_TQ, MLA_TQ), 1)
    diag_mask = (ki // CHUNK) <= (qi_ // CHUNK)

    def q_body(qi, carry):
        q0 = pl.multiple_of(qi * MLA_TQ, MLA_TQ)

        def score(kj):
            k0 = pl.multiple_of(kj * MLA_TQ, MLA_TQ)
            return tuple(_dot(k_s[pl.ds(k0, MLA_TQ), hh * HEAD_PAD:(hh + 1) * HEAD_PAD],
                              qt_s[qi, hh * HEAD_PAD:(hh + 1) * HEAD_PAD, :])
                         for hh in range(MLA_HEADS))

        def attend(kj, scores, state, masked):
            stats, probs = [], []
            for hh, s in enumerate(scores):
                m, l, _ = state[hh]
                if masked:
                    s = jnp.where(diag_mask, s, -jnp.inf)
                m_new = jnp.maximum(m, jnp.max(s, axis=0, keepdims=True))
                alpha = jnp.exp(m - m_new)
                p = jnp.exp(s - m_new)
                stats.append((m_new, alpha * l + jnp.sum(p, axis=0, keepdims=True), alpha))
                probs.append(p.astype(BF16))
            return tuple(
                (m_new, l, alpha * state[hh][2] + _dot(vt_s[kj, hh * MLA_V:(hh + 1) * MLA_V, :], probs[hh]))
                for hh, (m_new, l, alpha) in enumerate(stats))

        init = tuple((jnp.full((1, MLA_TQ), -jnp.inf, F32), jnp.zeros((1, MLA_TQ), F32),
                      jnp.zeros((MLA_V, MLA_TQ), F32)) for _ in range(MLA_HEADS))
        state = attend(qi, score(qi), init, True)
        state = lax.fori_loop(0, qi, lambda kj, st: attend(kj, score(kj), st, False), state)
        out_t = jnp.concatenate([acc / l for (_, l, acc) in state], axis=0)
        o_ref[0, pl.ds(q0, MLA_TQ), :] = out_t.T.astype(o_ref.dtype)
        return carry

    lax.fori_loop(0, n_qblk, q_body, 0)


def _mla_weights(w_qb, w_kvb):
    qr = w_qb.reshape(MLA_Q_RANK, MLA_HEADS, MLA_NOPE + MLA_ROPE)
    zq = jnp.zeros((MLA_Q_RANK, MLA_HEADS, HEAD_PAD - MLA_NOPE - MLA_ROPE), w_qb.dtype)
    wq = jnp.concatenate([qr, zq], axis=2).reshape(MLA_Q_RANK, MLA_HEADS * HEAD_PAD)
    kvr = w_kvb.reshape(MLA_KV_RANK, MLA_HEADS, MLA_NOPE + MLA_V)
    zk = jnp.zeros((MLA_KV_RANK, MLA_HEADS, HEAD_PAD - MLA_NOPE), w_kvb.dtype)
    wk = jnp.concatenate([kvr[:, :, :MLA_NOPE], zk], axis=2).reshape(MLA_KV_RANK, MLA_HEADS * HEAD_PAD)
    wvt = kvr[:, :, MLA_NOPE:].reshape(MLA_KV_RANK, MLA_HEADS * MLA_V).T
    return wq.T.astype(BF16), wk.astype(BF16), wvt.astype(BF16)


def _mla_mixer(proj, positions, q_norm_g, w_qb, kv_norm_g, w_kvb):
    bsz, seq, _ = proj.shape
    inv_freq = ROPE_THETA ** (-jnp.arange(0, MLA_ROPE, 2, dtype=F32) / MLA_ROPE)
    invf = inv_freq.reshape(PE_HALF, 1)
    wqt, wk, wvt = _mla_weights(w_qb, w_kvb)
    wide = MLA_HEADS * HEAD_PAD
    vw = MLA_HEADS * MLA_V
    row = lambda shape: pl.BlockSpec(shape, lambda b: (0,) * len(shape))
    return pl.pallas_call(
        _mla_kernel,
        grid=(bsz,),
        in_specs=[pl.BlockSpec((1, seq, MLA_Q_RANK), lambda b: (b, 0, COL_MQ // MLA_Q_RANK)),
                  pl.BlockSpec((1, seq, MLA_KV_RANK), lambda b: (b, 0, COL_MKV // MLA_KV_RANK)),
                  pl.BlockSpec((1, seq, LANES), lambda b: (b, 0, COL_MISC // LANES)),
                  pl.BlockSpec((1, seq // SLAB, 1, SLAB), lambda b: (b, 0, 0, 0)),
                  row((PE_HALF, 1)), row((1, MLA_Q_RANK)), row((1, MLA_KV_RANK)),
                  row((wide, MLA_Q_RANK)), row((MLA_KV_RANK, wide)), row((vw, MLA_KV_RANK))],
        out_specs=pl.BlockSpec((1, seq, vw), lambda b: (b, 0, 0)),
        out_shape=jax.ShapeDtypeStruct((bsz, seq, vw), BF16),
        scratch_shapes=[pltpu.VMEM((seq // SLAB, wide, SLAB), BF16), pltpu.VMEM((seq, wide), BF16),
                        pltpu.VMEM((seq // SLAB, vw, SLAB), BF16)],
        compiler_params=_cparams(1),
        name="mla_mixer",
    )(proj, proj, proj, positions.reshape(bsz, seq // SLAB, 1, SLAB), invf,
      q_norm_g.astype(F32).reshape(1, MLA_Q_RANK), kv_norm_g.astype(F32).reshape(1, MLA_KV_RANK), wqt, wk, wvt)


FF_TILE = 1024


def _order_after(x, anchor):
    zero_bits = lax.shift_right_logical(pltpu.bitcast(anchor, jnp.uint32), jnp.uint32(32))
    return x + pltpu.bitcast(zero_bits, F32)


def _rg_front(rx, tail, convw_ref, convb_ref, wrg_ref, xc_s, pre_s):
    convw = convw_ref[...]
    for s0 in range(0, rx.shape[0], SLAB):
        prev = tail if s0 == 0 else rx[s0 - SUBLANES:s0]
        xc = _conv_taps(prev, rx[s0:s0 + SLAB], convw) + convb_ref[...]
        xc_s[s0:s0 + SLAB, :] = xc
        for gi, pre in enumerate(_rg_gates(xc, wrg_ref)):
            pre_s[s0:s0 + SLAB, gi * 2 * LANES:(gi + 1) * 2 * LANES] = pre


def _rg_back(xc_s, pre_s, gate, anchor, ba_ref, bx_ref, lam_ref, hc_s, ob_s):
    for s0 in range(0, xc_s.shape[0], SLAB):
        pres = [pre_s[s0:s0 + SLAB, gi * 2 * LANES:(gi + 1) * 2 * LANES] for gi in range(RG_WIDTH // LANES)]
        if anchor is not None:
            pres = [_order_after(p, anchor) for p in pres]
        outs = _rg_scan(xc_s[s0:s0 + SLAB, :], pres, gate[s0:s0 + SLAB], ba_ref, bx_ref, lam_ref, hc_s)
        for gi, o in enumerate(outs):
            ob_s[s0:s0 + SLAB, gi * LANES:(gi + 1) * LANES] = o.astype(ob_s.dtype)


def _outmlp_kernel(h_ref, oa_ref, oc_ref, gtm_ref, shf_ref, scf_ref, gtf_ref, g_ref, fg_ref,
                   wo_ref, w1_ref, w2_ref,
                   rx0_ref, gate0_ref, rxn_ref, gaten_ref, tailn_ref,
                   convw_ref, convb_ref, wrg_ref, ba_ref, bx_ref, lam_ref,
                   o_ref, ob_s, hc_s, xc_s, pre_s, *, final_norm):
    n_i = pl.num_programs(1)
    step = pl.program_id(0) * n_i + pl.program_id(1)
    front_refs = (convw_ref, convb_ref, wrg_ref, xc_s, pre_s)
    back_refs = (ba_ref, bx_ref, lam_ref, hc_s, ob_s)

    @pl.when(step == 0)
    def _():
        hc_s[...] = jnp.zeros_like(hc_s)
        _rg_front(rx0_ref[0], jnp.zeros((SUBLANES, RG_WIDTH), F32), *front_refs)
        _rg_back(xc_s, pre_s, gate0_ref[0], None, *back_refs)

    seq_head = (step + 1) % n_i == 0
    hc_s[...] = jnp.where(seq_head, 0.0, hc_s[...])
    _rg_front(rxn_ref[0], jnp.where(seq_head, 0.0, tailn_ref[0]), *front_refs)
    na, nb = oa_ref.shape[2], ob_s.shape[1]
    mix = (_dot(oa_ref[0], wo_ref[0:na, :]) + _dot(ob_s[...], wo_ref[na:na + nb, :])
           + _dot(oc_ref[0], wo_ref[na + nb:, :]))
    h1 = h_ref[0] + gtm_ref[0] * mix
    u = _modnorm(h1, g_ref[...], scf_ref[0], shf_ref[0]).astype(BF16)
    d_ff = w1_ref.shape[1]
    f = jnp.zeros_like(h1)
    anchor = None
    for t in range(d_ff // FF_TILE):
        hid = jnp.maximum(_dot(u, w1_ref[:, t * FF_TILE:(t + 1) * FF_TILE]), 0.0)
        if t == 0:
            anchor = hid[0:1, 0:2 * LANES]
        f = f + _dot((hid * hid).astype(BF16), w2_ref[t * FF_TILE:(t + 1) * FF_TILE, :])
    h2 = h1 + gtf_ref[0] * f
    if final_norm:
        ms = jnp.mean(h2 * h2, axis=-1, keepdims=True)
        h2 = h2 * lax.rsqrt(ms + EPS) * fg_ref[...]
    o_ref[0] = h2
    _rg_back(xc_s, pre_s, gaten_ref[0], anchor, *back_refs)


def _out_mlp(h, o_a, o_c, proj, mod_l, g, final_g, w_out, w1, w2, rg_params, final_norm, tm=512):
    bsz, seq, d = h.shape
    d_ff = w1.shape[1]
    n_i = seq // tm
    last = bsz * n_i - 1
    mod3 = mod_l.reshape(bsz, 1, N_MOD * d)
    conv_w, conv_b, w_a, b_a, w_x, b_x, lam = rg_params
    r2 = lambda v: v.astype(F32).reshape(1, RG_WIDTH)
    modspec = lambda m: pl.BlockSpec((1, 1, d), lambda b, i: (b, 0, m))
    tile = lambda n: pl.BlockSpec((1, tm, n), lambda b, i: (b, i, 0))
    const = lambda shape: pl.BlockSpec(shape, lambda b, i: (0,) * len(shape), pipeline_mode=pl.Buffered(1))

    def nxt(b, i):
        lin = jnp.minimum(b * n_i + i + 1, last)
        return lin // n_i, lin % n_i

    def next_tile(col):
        return pl.BlockSpec((1, tm, RG_WIDTH), lambda b, i: (*nxt(b, i), col // RG_WIDTH))

    def next_tail(b, i):
        nb, ni = nxt(b, i)
        return nb, jnp.maximum(ni * (tm // SUBLANES) - 1, 0), COL_RX // RG_WIDTH

    first_tile = lambda col: pl.BlockSpec((1, tm, RG_WIDTH), lambda b, i: (0, 0, col // RG_WIDTH))
    return pl.pallas_call(
        functools.partial(_outmlp_kernel, final_norm=final_norm),
        grid=(bsz, n_i),
        in_specs=[tile(d), tile(o_a.shape[2]), tile(o_c.shape[2]),
                  modspec(2), modspec(3), modspec(4), modspec(5),
                  const((1, d)), const((1, d)),
                  const((d, d)), const((d, d_ff)), const((d_ff, d)),
                  first_tile(COL_RX), first_tile(COL_RGATE), next_tile(COL_RX), next_tile(COL_RGATE),
                  pl.BlockSpec((1, SUBLANES, RG_WIDTH), next_tail),
                  const((RG_CONV, RG_WIDTH)), const((1, RG_WIDTH)),
                  const((RG_WIDTH // LANES, LANES, 2 * LANES)),
                  const((1, RG_WIDTH)), const((1, RG_WIDTH)), const((1, RG_WIDTH))],
        out_specs=tile(d),
        out_shape=jax.ShapeDtypeStruct((bsz, seq, d), F32),
        scratch_shapes=[pltpu.VMEM((tm, RG_WIDTH), BF16),
                        pltpu.VMEM((SUBLANES, RG_WIDTH), F32),
                        pltpu.VMEM((tm, RG_WIDTH), F32),
                        pltpu.VMEM((tm, 2 * RG_WIDTH), F32)],
        compiler_params=_cparams(2),
        name="outproj_mlp_rglru",
    )(h, o_a, o_c, mod3, mod3, mod3, mod3, g.reshape(1, d), final_g.reshape(1, d), w_out, w1, w2,
      proj, proj, proj, proj, proj,
      conv_w, r2(conv_b), _rg_weights(w_a, w_x), r2(b_a), r2(b_x), r2(lam))


def _pad_w_in(w_in_l):
    d = w_in_l.shape[0]
    o = 0
    parts = {}
    for name, n in (("gqkvz", 4 * GDN_W), ("ga", GDN_HEADS), ("gb", GDN_HEADS), ("rx", RG_WIDTH),
                    ("rgate", RG_WIDTH), ("mq", MLA_Q_RANK), ("mkv", MLA_KV_RANK), ("mkr", MLA_ROPE)):
        parts[name] = w_in_l[:, o:o + n]
        o += n
    pad = jnp.zeros((d, NP_IN - COL_MISC - MLA_ROPE - 2 * GDN_HEADS), w_in_l.dtype)
    return jnp.concatenate([parts["gqkvz"], parts["rx"], parts["rgate"], parts["mq"], parts["mkv"],
                            parts["mkr"], parts["ga"], parts["gb"], pad], axis=1).astype(BF16)


def kernel(x, c, positions, w_mod, b_mod, norm_mix_g, w_in, gdn_conv_w, gdn_a_log, gdn_dt_bias, gdn_norm_g, rg_conv_w, rg_conv_b, rg_w_a, rg_b_a, rg_w_x, rg_b_x, rg_lambda, mla_q_norm_g, mla_w_qb, mla_kv_norm_g, mla_w_kvb, w_out, norm_mlp_g, w_mlp_in, w_mlp_out, final_norm_g):
    depth = w_mod.shape[0]
    mod = _modulation(c, w_mod, b_mod)
    h = x
    for l in range(depth):
        rg_params = (rg_conv_w[l], rg_conv_b[l], rg_w_a[l], rg_b_a[l], rg_w_x[l], rg_b_x[l], rg_lambda[l])
        proj = _in_projection(h, mod[l], norm_mix_g[l], _pad_w_in(w_in[l]))
        o_a = _gdn_mixer(proj, gdn_conv_w[l], gdn_a_log[l], gdn_dt_bias[l], gdn_norm_g[l])
        o_c = _mla_mixer(proj, positions, mla_q_norm_g[l], mla_w_qb[l], mla_kv_norm_g[l], mla_w_kvb[l])
        h = _out_mlp(h, o_a, o_c, proj, mod[l], norm_mlp_g[l], final_norm_g,
                     w_out[l].astype(BF16), w_mlp_in[l].astype(BF16), w_mlp_out[l].astype(BF16),
                     rg_params, final_norm=(l == depth - 1))
    return h
```

```python
import functools

import jax
import jax.numpy as jnp
from jax import lax
from jax.experimental import pallas as pl
from jax.experimental.pallas import tpu as pltpu

F32 = jnp.float32
BF16 = jnp.bfloat16

EPS = 1e-6
N_MOD = 6
CHUNK = 64
GDN_HEADS = 4
GDN_DK = 64
GDN_W = GDN_HEADS * GDN_DK
GDN_CONV = 4
RG_WIDTH = 512
RG_BLOCK = 64
RG_CONV = 4
RG_C = 8.0
MLA_HEADS = 4
MLA_NOPE = 64
MLA_ROPE = 32
MLA_V = 64
MLA_Q_RANK = 256
MLA_KV_RANK = 128
ROPE_THETA = 10000.0
LANES = 128
SUBLANES = 8
HEAD_PAD = LANES

COL_GDN = 0
COL_RX = 1024
COL_RGATE = 1536
COL_MQ = 2048
COL_MKV = 2304
COL_MISC = 2432
NP_IN = 2560
MISC_A = MLA_ROPE
MISC_B = MLA_ROPE + GDN_HEADS

SLAB = 256
GDN_SLAB = 512
VMEM_LIMIT = 56 * 1024 * 1024


def _cparams(n_axes):
    return pltpu.CompilerParams(dimension_semantics=("arbitrary",) * n_axes,
                                vmem_limit_bytes=VMEM_LIMIT)


def _dot(a, b):
    return jnp.dot(a, b, preferred_element_type=F32)


def _dot_nt(a, b):
    return lax.dot_general(a, b, (((1,), (1,)), ((), ())), preferred_element_type=F32)


def _split3(x):
    hi = x.astype(BF16)
    r1 = x - hi.astype(F32)
    mid = r1.astype(BF16)
    lo = (r1 - mid.astype(F32)).astype(BF16)
    return hi, mid, lo


def _dot3_lhs(x, w):
    hi, mid, lo = _split3(x)
    return _dot(hi, w) + _dot(mid, w) + _dot(lo, w)


def _dot3_rhs(w, x):
    hi, mid, lo = _split3(x)
    return _dot(w, hi) + _dot(w, mid) + _dot(w, lo)


def _sigmoid(x):
    return 1.0 / (1.0 + jnp.exp(-x))


def _silu(x):
    return x * _sigmoid(x)


def _softplus(x):
    return jnp.maximum(x, 0.0) + jnp.log1p(jnp.exp(-jnp.abs(x)))


def _conv_taps(prev, cur, w):
    taps = w.shape[0]
    xcat = jnp.concatenate([prev, cur], axis=0)
    acc = cur * w[taps - 1:taps, :]
    for j in range(taps - 1):
        shifted = pltpu.roll(xcat, taps - 1 - j, axis=0)[SUBLANES:, :]
        acc = acc + shifted * w[j:j + 1, :]
    return acc


def _causal_conv(ref, r0, rows, col0, ncols, w):
    cur = ref[0, pl.ds(r0, rows), col0:col0 + ncols]
    pstart = pl.multiple_of(jnp.maximum(r0 - SUBLANES, 0), SUBLANES)
    prev = ref[0, pl.ds(pstart, SUBLANES), col0:col0 + ncols]
    return _conv_taps(jnp.where(r0 > 0, prev, 0.0), cur, w)


def _mod_kernel(c_ref, w_ref, b_ref, o_ref):
    ca = _silu(c_ref[...])
    o_ref[0] = _dot(ca.astype(BF16), w_ref[0].astype(BF16)) + b_ref[0]


def _modulation(c, w_mod, b_mod):
    depth, d, n = w_mod.shape
    bsz = c.shape[0]
    tn = 1536
    return pl.pallas_call(
        _mod_kernel,
        grid=(depth, n // tn),
        in_specs=[pl.BlockSpec((bsz, d), lambda l, j: (0, 0)),
                  pl.BlockSpec((1, d, tn), lambda l, j: (l, 0, j)),
                  pl.BlockSpec((1, 1, tn), lambda l, j: (l, 0, j))],
        out_specs=pl.BlockSpec((1, bsz, tn), lambda l, j: (l, 0, j)),
        out_shape=jax.ShapeDtypeStruct((depth, bsz, n), F32),
        compiler_params=_cparams(2),
        name="adaln_mod",
    )(c, w_mod, b_mod.reshape(depth, 1, n))


def _modnorm(x, g, sc, sh):
    ms = jnp.mean(x * x, axis=-1, keepdims=True)
    return (x * lax.rsqrt(ms + EPS) * g) * (1.0 + sc) + sh


def _gelu_tanh(x):
    return 0.5 * x * (1.0 + jnp.tanh(0.7978845608028654 * (x + 0.044715 * (x * x * x))))


def _rg_gates(xc, w_ref):
    return [_dot(xc[:, gi * LANES:(gi + 1) * LANES].astype(BF16), w_ref[gi]) for gi in range(RG_WIDTH // LANES)]


def _rg_scan(xc_all, pres, gate, ba_ref, bx_ref, lam_ref, hc_s):
    n_tile = SLAB // SUBLANES
    sub = lax.broadcasted_iota(jnp.int32, (n_tile, SUBLANES, LANES), 1)
    outs = []
    for gi, pre in enumerate(pres):
        l0, l1 = gi * LANES, (gi + 1) * LANES
        if outs:
            done = outs[-1][0:1, :]
            pre = _order_after(pre, jnp.concatenate([done, done], axis=1))
        xc = xc_all[:, l0:l1]
        r = _sigmoid(pre[:, :LANES] + ba_ref[:, l0:l1])
        ig = _sigmoid(pre[:, LANES:] + bx_ref[:, l0:l1])
        log_a = (-RG_C) * r * _softplus(-lam_ref[:, l0:l1])
        a = jnp.exp(log_a)
        om = 1.0 - a * a
        mult = jnp.where(om > 0.0, om * lax.rsqrt(om), 0.0)
        bt = mult * (ig * xc)
        a3 = a.reshape(n_tile, SUBLANES, LANES)
        b3 = bt.reshape(n_tile, SUBLANES, LANES)
        d = 1
        while d < SUBLANES:
            a_sh = pltpu.roll(a3, d, axis=1)
            b_sh = pltpu.roll(b3, d, axis=1)
            keep = sub >= d
            b3 = jnp.where(keep, a3 * b_sh + b3, b3)
            a3 = jnp.where(keep, a3 * a_sh, a3)
            d *= 2
        car = hc_s[:, l0:l1]
        hs = []
        for n in range(n_tile):
            hn = a3[n] * car + b3[n]
            hs.append(hn)
            car = jnp.broadcast_to(hn[SUBLANES - 1:SUBLANES, :], (SUBLANES, LANES))
        hc_s[:, l0:l1] = car
        outs.append(jnp.concatenate(hs, axis=0) * _gelu_tanh(gate[:, l0:l1]))
    return outs


def _inproj_kernel(h_ref, sh_ref, sc_ref, g_ref, w_ref, o_ref):
    u = _modnorm(h_ref[0], g_ref[...], sc_ref[0], sh_ref[0])
    o_ref[0] = _dot(u.astype(BF16), w_ref[...])


def _in_projection(h, mod_l, g, w_pad, tm=512):
    bsz, seq, d = h.shape
    npad = w_pad.shape[1]
    mod3 = mod_l.reshape(bsz, 1, N_MOD * d)
    return pl.pallas_call(
        _inproj_kernel,
        grid=(bsz, seq // tm),
        in_specs=[pl.BlockSpec((1, tm, d), lambda b, i: (b, i, 0)),
                  pl.BlockSpec((1, 1, d), lambda b, i: (b, 0, 0)),
                  pl.BlockSpec((1, 1, d), lambda b, i: (b, 0, 1)),
                  pl.BlockSpec((1, d), lambda b, i: (0, 0)),
                  pl.BlockSpec((d, npad), lambda b, i: (0, 0))],
        out_specs=pl.BlockSpec((1, tm, npad), lambda b, i: (b, i, 0)),
        out_shape=jax.ShapeDtypeStruct((bsz, seq, npad), F32),
        compiler_params=_cparams(2),
        name="norm_inproj",
    )(h, mod3, mod3, g.reshape(1, d), w_pad)


def _head_of(idx):
    return idx // GDN_DK


def _blockdiag(x):
    xb = x.astype(BF16)
    lane_head = _head_of(lax.broadcasted_iota(jnp.int32, xb.shape, 1))
    zero = jnp.zeros_like(xb)
    return jnp.concatenate([jnp.where(lane_head == hh, xb, zero) for hh in range(GDN_HEADS)], axis=0)


def _weave(*stages, start=None):
    start = start or [0] * len(stages)
    results = [None] * len(stages)
    live = list(range(len(stages)))
    turn = 0
    while live:
        for idx in list(live):
            if turn < start[idx]:
                continue
            try:
                next(stages[idx])
            except StopIteration as stop:
                results[idx] = stop.value
                live.remove(idx)
        turn += 1
    return results


def _gdn_kernel(qkvz_ref, misc_ref, convw_ref, nalog_ref, dtb_ref, ng_ref, o_ref, o_s, st_s):
    seq = qkvz_ref.shape[1]
    n_slab = seq // GDN_SLAB
    per_slab = GDN_SLAB // CHUNK
    per_part = SLAB // CHUNK

    ii = lax.broadcasted_iota(jnp.int32, (GDN_W, GDN_W), 0)
    jj = lax.broadcasted_iota(jnp.int32, (GDN_W, GDN_W), 1)
    same_blk = _head_of(ii) == _head_of(jj)
    ones_bd = jnp.where(same_blk, 1.0, 0.0).astype(BF16)
    tri_bd = jnp.where(same_blk & (ii >= jj), 1.0, 0.0).astype(BF16)
    mi = lax.broadcasted_iota(jnp.int32, (LANES, GDN_W), 0)
    mj = lax.broadcasted_iota(jnp.int32, (LANES, GDN_W), 1)
    sel_a = jnp.where(mi == MISC_A + _head_of(mj), 1.0, 0.0).astype(BF16)
    sel_b = jnp.where(mi == MISC_B + _head_of(mj), 1.0, 0.0).astype(BF16)
    ci = lax.broadcasted_iota(jnp.int32, (CHUNK, GDN_W), 0)
    cj = lax.broadcasted_iota(jnp.int32, (CHUNK, GDN_W), 1)
    cs = cj % CHUNK
    lower_incl = ci >= cs
    strict_lower = ci > cs
    eye_hl = jnp.where(ci == cs, 1.0, 0.0)
    convw = convw_ref[...]
    rng = [(c * CHUNK, (c + 1) * CHUNK) for c in range(per_slab)]

    def slab_stage(si):
        parts = []
        for p in range(GDN_SLAB // SLAB):
            parts.append((yield from part_stage(pl.multiple_of(si * GDN_SLAB + p * SLAB, SLAB))))
        rows = lambda i: jnp.concatenate([part[i] for part in parts], axis=0)
        return tuple(rows(i) for i in range(7)) + tuple(sum((part[i] for part in parts), []) for i in (7, 8))

    def part_stage(r0):
        y = _silu(_causal_conv(qkvz_ref, r0, SLAB, 0, 3 * GDN_W, convw))
        q, k, v = y[:, :GDN_W], y[:, GDN_W:2 * GDN_W], y[:, 2 * GDN_W:]

        def l2n(x):
            xx = x * x
            hi = xx.astype(BF16)
            lo = (xx - hi.astype(F32)).astype(BF16)
            ss = _dot(hi, ones_bd) + _dot(lo, ones_bd)
            return x * lax.rsqrt(ss + EPS)

        q = l2n(q) * (GDN_DK ** -0.5)
        k = l2n(k)
        yield
        misc = misc_ref[0, pl.ds(r0, SLAB), :]
        a_exp = _dot3_lhs(misc, sel_a)
        b_exp = _dot3_lhs(misc, sel_b)
        g = nalog_ref[...] * _softplus(a_exp + dtb_ref[...])
        beta = _sigmoid(b_exp)
        yield
        gcum = _dot3_rhs(tri_bd, g)
        glast = _dot3_rhs(ones_bd, g)
        eg = jnp.exp(gcum)
        kb = k * beta
        vb = (v * beta).astype(BF16)
        kbg = (kb * eg).astype(BF16)
        qd = (q * eg).astype(BF16)
        kt = k * jnp.exp(glast - gcum)
        cdec = jnp.exp(glast)
        ktts = []
        for a, b in rng[:per_part]:
            ktc = jnp.concatenate([kt[a:b], jnp.zeros((LANES - CHUNK, GDN_W), F32)], axis=0)
            ktts.append(ktc.T.astype(BF16))
        yield
        return (q.astype(BF16), k.astype(BF16), kb.astype(BF16), vb, kbg, qd, gcum, ktts,
                [cdec[a:a + 1] for a, _ in rng[:per_part]])

    def chunk_stage(arrs):
        q, k, kb, vb, kbg, qd, gcum, ktts, cds = arrs
        aqs = [_dot_nt(jnp.concatenate([kb[a:b], q[a:b]], axis=0), _blockdiag(k[a:b])) for a, b in rng]
        yield

        g_t = [gcum[:, :LANES].T, gcum[:, LANES:].T]
        head_rows = [g_t[hh // 2][(hh % 2) * GDN_DK:(hh % 2) * GDN_DK + 1, :] for hh in range(GDN_HEADS)]
        grs = [jnp.concatenate([row[:, a:b] for row in head_rows], axis=1) for a, b in rng]
        decays = [jnp.exp(jnp.where(lower_incl, gcum[a:b] - gr, 0.0)) for (a, b), gr in zip(rng, grs)]
        yield
        lmats = [jnp.where(strict_lower, aq[:CHUNK] * dec, 0.0) for aq, dec in zip(aqs, decays)]
        qks = [jnp.where(lower_incl, aq[CHUNK:] * dec, 0.0).astype(BF16) for aq, dec in zip(aqs, decays)]
        dinvs = [eye_hl - jnp.where(ci // 2 == cs // 2, lm, 0.0) for lm in lmats]
        blk = 2
        while blk < CHUNK:
            off_diag = (ci // (2 * blk) == cs // (2 * blk)) & (ci // blk != cs // blk)
            ps = [_dot(d.astype(BF16), _blockdiag(jnp.where(off_diag, lm, 0.0))) for d, lm in zip(dinvs, lmats)]
            yield
            dinvs = [d - _dot(p.astype(BF16), _blockdiag(d)) for d, p in zip(dinvs, ps)]
            yield
            blk *= 2
        ops = []
        for (a, b), d, qk, ktt, cd in zip(rng, dinvs, qks, ktts, cds):
            tb = d.astype(BF16)
            u = _dot(tb, _blockdiag(vb[a:b]))
            w = _dot(tb, _blockdiag(kbg[a:b])).astype(BF16)
            ops.append((u, w, qd[a:b], qk, ktt, cd))
        return ops

    def recur(ops, r_base):
        for c, (u, w, qd, qk, ktt, cd) in enumerate(ops):
            wq = _dot(jnp.concatenate([w, qd], axis=0), st_s[...].astype(BF16))
            yield
            v_new = u - wq[:CHUNK]
            o_s[pl.ds(r_base + c * CHUNK, CHUNK), :] = wq[CHUNK:] + _dot(qk, _blockdiag(v_new))
            vpad = jnp.concatenate([v_new.astype(BF16), jnp.zeros((LANES - CHUNK, GDN_W), BF16)], axis=0)
            upd = _dot(ktt, vpad)
            yield
            st_s[...] = st_s[...] * cd + jnp.where(same_blk, upd, 0.0)
            yield

    def prepare(si):
        arrs = yield from slab_stage(si)
        return (yield from chunk_stage(arrs))

    def slab_body(si, prev_ops):
        r_prev = pl.multiple_of(jnp.maximum(si - 1, 0) * GDN_SLAB, GDN_SLAB)
        return _weave(recur(prev_ops, r_prev), prepare(si))[1]

    idle = [(jnp.zeros((CHUNK, GDN_W), F32), jnp.zeros((CHUNK, GDN_W), BF16), jnp.zeros((CHUNK, GDN_W), BF16),
             jnp.zeros((CHUNK, GDN_W), BF16), jnp.zeros((GDN_W, LANES), BF16), jnp.ones((1, GDN_W), F32))
            for _ in range(per_slab)]
    st_s[...] = jnp.zeros_like(st_s)
    last_ops = lax.fori_loop(0, n_slab, slab_body, idle)

    mean_bd = jnp.where(same_blk, 1.0 / GDN_DK, 0.0).astype(BF16)

    def gated_norm(r0):
        o = o_s[pl.ds(r0, SLAB), :]
        oo = o * o
        hi = oo.astype(BF16)
        lo = (oo - hi.astype(F32)).astype(BF16)
        ms = _dot(hi, mean_bd) + _dot(lo, mean_bd)
        y = o * lax.rsqrt(ms + EPS) * ng_ref[...]
        z = qkvz_ref[0, pl.ds(r0, SLAB), 3 * GDN_W:4 * GDN_W]
        o_ref[0, pl.ds(r0, SLAB), :] = (y * _silu(z)).astype(o_ref.dtype)

    last_r0 = (n_slab - 1) * GDN_SLAB

    def finished_slabs():
        for r0 in range(0, last_r0, SLAB):
            gated_norm(r0)
            yield

    _weave(recur(last_ops, last_r0), finished_slabs())
    for r0 in range(last_r0, seq, SLAB):
        gated_norm(r0)


def _gdn_mixer(proj, conv_w, a_log, dt_bias, norm_g):
    bsz, seq, _ = proj.shape
    nalog = jnp.repeat(-jnp.exp(a_log.astype(F32)), GDN_DK).reshape(1, GDN_W)
    dtb = jnp.repeat(dt_bias.astype(F32), GDN_DK).reshape(1, GDN_W)
    ng = jnp.tile(norm_g.astype(F32), GDN_HEADS).reshape(1, GDN_W)
    row = lambda shape: pl.BlockSpec(shape, lambda b: (0,) * len(shape))
    return pl.pallas_call(
        _gdn_kernel,
        grid=(bsz,),
        in_specs=[pl.BlockSpec((1, seq, 4 * GDN_W), lambda b: (b, 0, COL_GDN // (4 * GDN_W))),
                  pl.BlockSpec((1, seq, LANES), lambda b: (b, 0, COL_MISC // LANES)),
                  row((GDN_CONV, 3 * GDN_W)), row((1, GDN_W)), row((1, GDN_W)), row((1, GDN_W))],
        out_specs=pl.BlockSpec((1, seq, GDN_W), lambda b: (b, 0, 0)),
        out_shape=jax.ShapeDtypeStruct((bsz, seq, GDN_W), BF16),
        scratch_shapes=[pltpu.VMEM((seq, GDN_W), F32),
                        pltpu.VMEM((GDN_W, GDN_W), F32)],
        compiler_params=_cparams(1),
        name="gdn_mixer",
    )(proj, proj, conv_w, nalog, dtb, ng)


def _rg_weights(w_a, w_x):
    def pair_bd(w):
        z = jnp.zeros((RG_BLOCK, RG_BLOCK), w.dtype)
        out = []
        for gi in range(RG_WIDTH // LANES):
            top = jnp.concatenate([w[2 * gi], z], axis=1)
            bot = jnp.concatenate([z, w[2 * gi + 1]], axis=1)
            out.append(jnp.concatenate([top, bot], axis=0))
        return jnp.stack(out)
    return jnp.concatenate([pair_bd(w_a), pair_bd(w_x)], axis=2).astype(BF16)


MLA_TQ = SLAB
PE0 = MLA_NOPE
PE_HALF = MLA_ROPE // 2


def _rms(x, g):
    ms = jnp.mean(x * x, axis=-1, keepdims=True)
    return x * lax.rsqrt(ms + EPS) * g


def _mla_kernel(mq_ref, mkv_ref, misc_ref, pos_ref, invf_ref, qg_ref, kvg_ref, wqt_ref, wk_ref, wvt_ref,
                o_ref, qt_s, k_s, vt_s):
    seq = mq_ref.shape[1]
    n_slab = seq // SLAB
    n_qblk = seq // MLA_TQ
    scale = (MLA_NOPE + MLA_ROPE) ** -0.5
    x1_0, x2_0, pe_end = PE0, PE0 + PE_HALF, PE0 + MLA_ROPE

    def prep_body(si, carry):
        r0 = pl.multiple_of(si * SLAB, SLAB)
        ang_t = invf_ref[...] * pos_ref[0, si].astype(F32)
        cos_t = jnp.cos(ang_t)
        sin_t = jnp.sin(ang_t)

        def rope_t(x1, x2):
            return x1 * cos_t - x2 * sin_t, x2 * cos_t + x1 * sin_t

        qn = _rms(mq_ref[0, pl.ds(r0, SLAB), :], qg_ref[...]).astype(BF16)
        q_t = _dot_nt(wqt_ref[...], qn)
        for hh in range(MLA_HEADS):
            blk = q_t[hh * HEAD_PAD:(hh + 1) * HEAD_PAD]
            x1, x2 = rope_t(blk[x1_0:x2_0], blk[x2_0:pe_end])
            roped = jnp.concatenate([blk[:PE0], x1, x2, blk[pe_end:]], axis=0)
            qt_s[si, hh * HEAD_PAD:(hh + 1) * HEAD_PAD, :] = (roped * scale).astype(BF16)
        kvn = _rms(mkv_ref[0, pl.ds(r0, SLAB), :], kvg_ref[...]).astype(BF16)
        kn = _dot(kvn, wk_ref[...])
        vt_s[si] = _dot_nt(wvt_ref[...], kvn).astype(BF16)
        misc_t = misc_ref[0, pl.ds(r0, SLAB), :].T
        k1, k2 = rope_t(misc_t[:PE_HALF], misc_t[PE_HALF:MLA_ROPE])
        kr = jnp.concatenate([jnp.zeros((PE0, SLAB), F32), k1, k2,
                              jnp.zeros((HEAD_PAD - pe_end, SLAB), F32)], axis=0).T
        for hh in range(MLA_HEADS):
            l0, l1 = hh * HEAD_PAD, (hh + 1) * HEAD_PAD
            k_s[pl.ds(r0, SLAB), l0:l1] = (kn[:, l0:l1] + kr).astype(BF16)
        return carry

    lax.fori_loop(0, n_slab, prep_body, 0)

    ki = lax.broadcasted_iota(jnp.int32, (MLA_TQ, MLA_TQ), 0)
    qi_ = lax.broadcasted_iota(jnp.int32, (MLA_TQ, MLA_TQ), 1)
    diag_mask = (ki // CHUNK) <= (qi_ // CHUNK)

    def q_body(qi, carry):
        q0 = pl.multiple_of(qi * MLA_TQ, MLA_TQ)

        def score(kj):
            k0 = pl.multiple_of(kj * MLA_TQ, MLA_TQ)
            return tuple(_dot(k_s[pl.ds(k0, MLA_TQ), hh * HEAD_PAD:(hh + 1) * HEAD_PAD],
                              qt_s[qi, hh * HEAD_PAD:(hh + 1) * HEAD_PAD, :])
                         for hh in range(MLA_HEADS))

        def attend(kj, scores, state, masked):
            stats, probs = [], []
            for hh, s in enumerate(scores):
                m, l, _ = state[hh]
                if masked:
                    s = jnp.where(diag_mask, s, -jnp.inf)
                m_new = jnp.maximum(m, jnp.max(s, axis=0, keepdims=True))
                alpha = jnp.exp(m - m_new)
                p = jnp.exp(s - m_new)
                stats.append((m_new, alpha * l + jnp.sum(p, axis=0, keepdims=True), alpha))
                probs.append(p.astype(BF16))
            return tuple(
                (m_new, l, alpha * state[hh][2] + _dot(vt_s[kj, hh * MLA_V:(hh + 1) * MLA_V, :], probs[hh]))
                for hh, (m_new, l, alpha) in enumerate(stats))

        init = tuple((jnp.full((1, MLA_TQ), -jnp.inf, F32), jnp.zeros((1, MLA_TQ), F32),
                      jnp.zeros((MLA_V, MLA_TQ), F32)) for _ in range(MLA_HEADS))
        state = attend(qi, score(qi), init, True)
        state = lax.fori_loop(0, qi, lambda kj, st: attend(kj, score(kj), st, False), state)
        out_t = jnp.concatenate([acc / l for (_, l, acc) in state], axis=0)
        o_ref[0, pl.ds(q0, MLA_TQ), :] = out_t.T.astype(o_ref.dtype)
        return carry

    lax.fori_loop(0, n_qblk, q_body, 0)


def _mla_weights(w_qb, w_kvb):
    qr = w_qb.reshape(MLA_Q_RANK, MLA_HEADS, MLA_NOPE + MLA_ROPE)
    zq = jnp.zeros((MLA_Q_RANK, MLA_HEADS, HEAD_PAD - MLA_NOPE - MLA_ROPE), w_qb.dtype)
    wq = jnp.concatenate([qr, zq], axis=2).reshape(MLA_Q_RANK, MLA_HEADS * HEAD_PAD)
    kvr = w_kvb.reshape(MLA_KV_RANK, MLA_HEADS, MLA_NOPE + MLA_V)
    zk = jnp.zeros((MLA_KV_RANK, MLA_HEADS, HEAD_PAD - MLA_NOPE), w_kvb.dtype)
    wk = jnp.concatenate([kvr[:, :, :MLA_NOPE], zk], axis=2).reshape(MLA_KV_RANK, MLA_HEADS * HEAD_PAD)
    wvt = kvr[:, :, MLA_NOPE:].reshape(MLA_KV_RANK, MLA_HEADS * MLA_V).T
    return wq.T.astype(BF16), wk.astype(BF16), wvt.astype(BF16)


def _mla_mixer(proj, positions, q_norm_g, w_qb, kv_norm_g, w_kvb):
    bsz, seq, _ = proj.shape
    inv_freq = ROPE_THETA ** (-jnp.arange(0, MLA_ROPE, 2, dtype=F32) / MLA_ROPE)
    invf = inv_freq.reshape(PE_HALF, 1)
    wqt, wk, wvt = _mla_weights(w_qb, w_kvb)
    wide = MLA_HEADS * HEAD_PAD
    vw = MLA_HEADS * MLA_V
    row = lambda shape: pl.BlockSpec(shape, lambda b: (0,) * len(shape))
    return pl.pallas_call(
        _mla_kernel,
        grid=(bsz,),
        in_specs=[pl.BlockSpec((1, seq, MLA_Q_RANK), lambda b: (b, 0, COL_MQ // MLA_Q_RANK)),
                  pl.BlockSpec((1, seq, MLA_KV_RANK), lambda b: (b, 0, COL_MKV // MLA_KV_RANK)),
                  pl.BlockSpec((1, seq, LANES), lambda b: (b, 0, COL_MISC // LANES)),
                  pl.BlockSpec((1, seq // SLAB, 1, SLAB), lambda b: (b, 0, 0, 0)),
                  row((PE_HALF, 1)), row((1, MLA_Q_RANK)), row((1, MLA_KV_RANK)),
                  row((wide, MLA_Q_RANK)), row((MLA_KV_RANK, wide)), row((vw, MLA_KV_RANK))],
        out_specs=pl.BlockSpec((1, seq, vw), lambda b: (b, 0, 0)),
        out_shape=jax.ShapeDtypeStruct((bsz, seq, vw), BF16),
        scratch_shapes=[pltpu.VMEM((seq // SLAB, wide, SLAB), BF16), pltpu.VMEM((seq, wide), BF16),
                        pltpu.VMEM((seq // SLAB, vw, SLAB), BF16)],
        compiler_params=_cparams(1),
        name="mla_mixer",
    )(proj, proj, proj, positions.reshape(bsz, seq // SLAB, 1, SLAB), invf,
      q_norm_g.astype(F32).reshape(1, MLA_Q_RANK), kv_norm_g.astype(F32).reshape(1, MLA_KV_RANK), wqt, wk, wvt)


FF_TILE = 1024


def _order_after(x, anchor):
    zero_bits = lax.shift_right_logical(pltpu.bitcast(anchor, jnp.uint32), jnp.uint32(32))
    return x + pltpu.bitcast(zero_bits, F32)


def _rg_front(rx, tail, convw_ref, convb_ref, wrg_ref, xc_s, pre_s):
    convw = convw_ref[...]
    for s0 in range(0, rx.shape[0], SLAB):
        prev = tail if s0 == 0 else rx[s0 - SUBLANES:s0]
        xc = _conv_taps(prev, rx[s0:s0 + SLAB], convw) + convb_ref[...]
        xc_s[s0:s0 + SLAB, :] = xc
        for gi, pre in enumerate(_rg_gates(xc, wrg_ref)):
            pre_s[s0:s0 + SLAB, gi * 2 * LANES:(gi + 1) * 2 * LANES] = pre


def _rg_back(xc_s, pre_s, gate, anchor, ba_ref, bx_ref, lam_ref, hc_s, ob_s):
    for s0 in range(0, xc_s.shape[0], SLAB):
        pres = [pre_s[s0:s0 + SLAB, gi * 2 * LANES:(gi + 1) * 2 * LANES] for gi in range(RG_WIDTH // LANES)]
        if anchor is not None:
            pres = [_order_after(p, anchor) for p in pres]
        outs = _rg_scan(xc_s[s0:s0 + SLAB, :], pres, gate[s0:s0 + SLAB], ba_ref, bx_ref, lam_ref, hc_s)
        for gi, o in enumerate(outs):
            ob_s[s0:s0 + SLAB, gi * LANES:(gi + 1) * LANES] = o.astype(ob_s.dtype)


def _outmlp_kernel(h_ref, oa_ref, oc_ref, gtm_ref, shf_ref, scf_ref, gtf_ref, g_ref, fg_ref,
                   wo_ref, w1_ref, w2_ref,
                   rx0_ref, gate0_ref, rxn_ref, gaten_ref, tailn_ref,
                   convw_ref, convb_ref, wrg_ref, ba_ref, bx_ref, lam_ref,
                   o_ref, ob_s, hc_s, xc_s, pre_s, *, final_norm):
    n_i = pl.num_programs(1)
    step = pl.program_id(0) * n_i + pl.program_id(1)
    front_refs = (convw_ref, convb_ref, wrg_ref, xc_s, pre_s)
    back_refs = (ba_ref, bx_ref, lam_ref, hc_s, ob_s)

    @pl.when(step == 0)
    def _():
        hc_s[...] = jnp.zeros_like(hc_s)
        _rg_front(rx0_ref[0], jnp.zeros((SUBLANES, RG_WIDTH), F32), *front_refs)
        _rg_back(xc_s, pre_s, gate0_ref[0], None, *back_refs)

    seq_head = (step + 1) % n_i == 0
    hc_s[...] = jnp.where(seq_head, 0.0, hc_s[...])
    _rg_front(rxn_ref[0], jnp.where(seq_head, 0.0, tailn_ref[0]), *front_refs)
    na, nb = oa_ref.shape[2], ob_s.shape[1]
    mix = (_dot(oa_ref[0], wo_ref[0:na, :]) + _dot(ob_s[...], wo_ref[na:na + nb, :])
           + _dot(oc_ref[0], wo_ref[na + nb:, :]))
    h1 = h_ref[0] + gtm_ref[0] * mix
    u = _modnorm(h1, g_ref[...], scf_ref[0], shf_ref[0]).astype(BF16)
    d_ff = w1_ref.shape[1]
    f = jnp.zeros_like(h1)
    anchor = None
    for t in range(d_ff // FF_TILE):
        hid = jnp.maximum(_dot(u, w1_ref[:, t * FF_TILE:(t + 1) * FF_TILE]), 0.0)
        if t == 0:
            anchor = hid[0:1, 0:2 * LANES]
        f = f + _dot((hid * hid).astype(BF16), w2_ref[t * FF_TILE:(t + 1) * FF_TILE, :])
    h2 = h1 + gtf_ref[0] * f
    if final_norm:
        ms = jnp.mean(h2 * h2, axis=-1, keepdims=True)
        h2 = h2 * lax.rsqrt(ms + EPS) * fg_ref[...]
    o_ref[0] = h2
    _rg_back(xc_s, pre_s, gaten_ref[0], anchor, *back_refs)


def _out_mlp(h, o_a, o_c, proj, mod_l, g, final_g, w_out, w1, w2, rg_params, final_norm, tm=512):
    bsz, seq, d = h.shape
    d_ff = w1.shape[1]
    n_i = seq // tm
    last = bsz * n_i - 1
    mod3 = mod_l.reshape(bsz, 1, N_MOD * d)
    conv_w, conv_b, w_a, b_a, w_x, b_x, lam = rg_params
    r2 = lambda v: v.astype(F32).reshape(1, RG_WIDTH)
    modspec = lambda m: pl.BlockSpec((1, 1, d), lambda b, i: (b, 0, m))
    tile = lambda n: pl.BlockSpec((1, tm, n), lambda b, i: (b, i, 0))
    const = lambda shape: pl.BlockSpec(shape, lambda b, i: (0,) * len(shape), pipeline_mode=pl.Buffered(1))

    def nxt(b, i):
        lin = jnp.minimum(b * n_i + i + 1, last)
        return lin // n_i, lin % n_i

    def next_tile(col):
        return pl.BlockSpec((1, tm, RG_WIDTH), lambda b, i: (*nxt(b, i), col // RG_WIDTH))

    def next_tail(b, i):
        nb, ni = nxt(b, i)
        return nb, jnp.maximum(ni * (tm // SUBLANES) - 1, 0), COL_RX // RG_WIDTH

    first_tile = lambda col: pl.BlockSpec((1, tm, RG_WIDTH), lambda b, i: (0, 0, col // RG_WIDTH))
    return pl.pallas_call(
        functools.partial(_outmlp_kernel, final_norm=final_norm),
        grid=(bsz, n_i),
        in_specs=[tile(d), tile(o_a.shape[2]), tile(o_c.shape[2]),
                  modspec(2), modspec(3), modspec(4), modspec(5),
                  const((1, d)), const((1, d)),
                  const((d, d)), const((d, d_ff)), const((d_ff, d)),
                  first_tile(COL_RX), first_tile(COL_RGATE), next_tile(COL_RX), next_tile(COL_RGATE),
                  pl.BlockSpec((1, SUBLANES, RG_WIDTH), next_tail),
                  const((RG_CONV, RG_WIDTH)), const((1, RG_WIDTH)),
                  const((RG_WIDTH // LANES, LANES, 2 * LANES)),
                  const((1, RG_WIDTH)), const((1, RG_WIDTH)), const((1, RG_WIDTH))],
        out_specs=tile(d),
        out_shape=jax.ShapeDtypeStruct((bsz, seq, d), F32),
        scratch_shapes=[pltpu.VMEM((tm, RG_WIDTH), BF16),
                        pltpu.VMEM((SUBLANES, RG_WIDTH), F32),
                        pltpu.VMEM((tm, RG_WIDTH), F32),
                        pltpu.VMEM((tm, 2 * RG_WIDTH), F32)],
        compiler_params=_cparams(2),
        name="outproj_mlp_rglru",
    )(h, o_a, o_c, mod3, mod3, mod3, mod3, g.reshape(1, d), final_g.reshape(1, d), w_out, w1, w2,
      proj, proj, proj, proj, proj,
      conv_w, r2(conv_b), _rg_weights(w_a, w_x), r2(b_a), r2(b_x), r2(lam))


def _pad_w_in(w_in_l):
    d = w_in_l.shape[0]
    o = 0
    parts = {}
    for name, n in (("gqkvz", 4 * GDN_W), ("ga", GDN_HEADS), ("gb", GDN_HEADS), ("rx", RG_WIDTH),
                    ("rgate", RG_WIDTH), ("mq", MLA_Q_RANK), ("mkv", MLA_KV_RANK), ("mkr", MLA_ROPE)):
        parts[name] = w_in_l[:, o:o + n]
        o += n
    pad = jnp.zeros((d, NP_IN - COL_MISC - MLA_ROPE - 2 * GDN_HEADS), w_in_l.dtype)
    return jnp.concatenate([parts["gqkvz"], parts["rx"], parts["rgate"], parts["mq"], parts["mkv"],
                            parts["mkr"], parts["ga"], parts["gb"], pad], axis=1).astype(BF16)


def kernel(x, c, positions, w_mod, b_mod, norm_mix_g, w_in, gdn_conv_w, gdn_a_log, gdn_dt_bias, gdn_norm_g, rg_conv_w, rg_conv_b, rg_w_a, rg_b_a, rg_w_x, rg_b_x, rg_lambda, mla_q_norm_g, mla_w_qb, mla_kv_norm_g, mla_w_kvb, w_out, norm_mlp_g, w_mlp_in, w_mlp_out, final_norm_g):
    depth = w_mod.shape[0]
    mod = _modulation(c, w_mod, b_mod)
    h = x
    for l in range(depth):
        rg_params = (rg_conv_w[l], rg_conv_b[l], rg_w_a[l], rg_b_a[l], rg_w_x[l], rg_b_x[l], rg_lambda[l])
        proj = _in_projection(h, mod[l], norm_mix_g[l], _pad_w_in(w_in[l]))
        o_a = _gdn_mixer(proj, gdn_conv_w[l], gdn_a_log[l], gdn_dt_bias[l], gdn_norm_g[l])
        o_c = _mla_mixer(proj, positions, mla_q_norm_g[l], mla_w_qb[l], mla_kv_norm_g[l], mla_w_kvb[l])
        h = _out_mlp(h, o_a, o_c, proj, mod[l], norm_mlp_g[l], final_norm_g,
                     w_out[l].astype(BF16), w_mlp_in[l].astype(BF16), w_mlp_out[l].astype(BF16),
                     rg_params, final_norm=(l == depth - 1))
    return h
```

```python
import functools

import jax
import jax.numpy as jnp
from jax import lax
from jax.experimental import pallas as pl
from jax.experimental.pallas import tpu as pltpu

F32 = jnp.float32
BF16 = jnp.bfloat16

EPS = 1e-6
N_MOD = 6
CHUNK = 64
GDN_HEADS = 4
GDN_DK = 64
GDN_W = GDN_HEADS * GDN_DK
GDN_CONV = 4
RG_WIDTH = 512
RG_BLOCK = 64
RG_CONV = 4
RG_C = 8.0
MLA_HEADS = 4
MLA_NOPE = 64
MLA_ROPE = 32
MLA_V = 64
MLA_Q_RANK = 256
MLA_KV_RANK = 128
ROPE_THETA = 10000.0
LANES = 128
SUBLANES = 8
HEAD_PAD = LANES

COL_GDN = 0
COL_RX = 1024
COL_RGATE = 1536
COL_MQ = 2048
COL_MKV = 2304
COL_MISC = 2432
NP_IN = 2560
MISC_A = MLA_ROPE
MISC_B = MLA_ROPE + GDN_HEADS

SLAB = 256
GDN_SLAB = 512
VMEM_LIMIT = 56 * 1024 * 1024


def _cparams(n_axes):
    return pltpu.CompilerParams(dimension_semantics=("arbitrary",) * n_axes,
                                vmem_limit_bytes=VMEM_LIMIT)


def _dot(a, b):
    return jnp.dot(a, b, preferred_element_type=F32)


def _dot_nt(a, b):
    return lax.dot_general(a, b, (((1,), (1,)), ((), ())), preferred_element_type=F32)


def _split3(x):
    hi = x.astype(BF16)
    r1 = x - hi.astype(F32)
    mid = r1.astype(BF16)
    lo = (r1 - mid.astype(F32)).astype(BF16)
    return hi, mid, lo


def _dot3_lhs(x, w):
    hi, mid, lo = _split3(x)
    return _dot(hi, w) + _dot(mid, w) + _dot(lo, w)


def _dot3_rhs(w, x):
    hi, mid, lo = _split3(x)
    return _dot(w, hi) + _dot(w, mid) + _dot(w, lo)


def _sigmoid(x):
    return 1.0 / (1.0 + jnp.exp(-x))


def _silu(x):
    return x * _sigmoid(x)


def _softplus(x):
    return jnp.maximum(x, 0.0) + jnp.log1p(jnp.exp(-jnp.abs(x)))


def _conv_taps(prev, cur, w):
    taps = w.shape[0]
    xcat = jnp.concatenate([prev, cur], axis=0)
    acc = cur * w[taps - 1:taps, :]
    for j in range(taps - 1):
        shifted = pltpu.roll(xcat, taps - 1 - j, axis=0)[SUBLANES:, :]
        acc = acc + shifted * w[j:j + 1, :]
    return acc


def _causal_conv(ref, r0, rows, col0, ncols, w):
    cur = ref[0, pl.ds(r0, rows), col0:col0 + ncols]
    pstart = pl.multiple_of(jnp.maximum(r0 - SUBLANES, 0), SUBLANES)
    prev = ref[0, pl.ds(pstart, SUBLANES), col0:col0 + ncols]
    return _conv_taps(jnp.where(r0 > 0, prev, 0.0), cur, w)


def _mod_kernel(c_ref, w_ref, b_ref, o_ref):
    ca = _silu(c_ref[...])
    o_ref[0] = _dot(ca.astype(BF16), w_ref[0].astype(BF16)) + b_ref[0]


def _modulation(c, w_mod, b_mod):
    depth, d, n = w_mod.shape
    bsz = c.shape[0]
    tn = 1536
    return pl.pallas_call(
        _mod_kernel,
        grid=(depth, n // tn),
        in_specs=[pl.BlockSpec((bsz, d), lambda l, j: (0, 0)),
                  pl.BlockSpec((1, d, tn), lambda l, j: (l, 0, j)),
                  pl.BlockSpec((1, 1, tn), lambda l, j: (l, 0, j))],
        out_specs=pl.BlockSpec((1, bsz, tn), lambda l, j: (l, 0, j)),
        out_shape=jax.ShapeDtypeStruct((depth, bsz, n), F32),
        compiler_params=_cparams(2),
        name="adaln_mod",
    )(c, w_mod, b_mod.reshape(depth, 1, n))


def _modnorm(x, g, sc, sh):
    ms = jnp.mean(x * x, axis=-1, keepdims=True)
    return (x * lax.rsqrt(ms + EPS) * g) * (1.0 + sc) + sh


def _gelu_tanh(x):
    return 0.5 * x * (1.0 + jnp.tanh(0.7978845608028654 * (x + 0.044715 * (x * x * x))))


def _rg_gates(xc, w_ref):
    return [_dot(xc[:, gi * LANES:(gi + 1) * LANES].astype(BF16), w_ref[gi]) for gi in range(RG_WIDTH // LANES)]


def _rg_scan(xc_all, pres, gate, ba_ref, bx_ref, lam_ref, hc_s):
    n_tile = SLAB // SUBLANES
    sub = lax.broadcasted_iota(jnp.int32, (n_tile, SUBLANES, LANES), 1)
    outs = []
    for gi, pre in enumerate(pres):
        l0, l1 = gi * LANES, (gi + 1) * LANES
        if outs:
            done = outs[-1][0:1, :]
            pre = _order_after(pre, jnp.concatenate([done, done], axis=1))
        xc = xc_all[:, l0:l1]
        r = _sigmoid(pre[:, :LANES] + ba_ref[:, l0:l1])
        ig = _sigmoid(pre[:, LANES:] + bx_ref[:, l0:l1])
        log_a = (-RG_C) * r * _softplus(-lam_ref[:, l0:l1])
        a = jnp.exp(log_a)
        om = 1.0 - a * a
        mult = jnp.where(om > 0.0, om * lax.rsqrt(om), 0.0)
        bt = mult * (ig * xc)
        a3 = a.reshape(n_tile, SUBLANES, LANES)
        b3 = bt.reshape(n_tile, SUBLANES, LANES)
        d = 1
        while d < SUBLANES:
            a_sh = pltpu.roll(a3, d, axis=1)
            b_sh = pltpu.roll(b3, d, axis=1)
            keep = sub >= d
            b3 = jnp.where(keep, a3 * b_sh + b3, b3)
            a3 = jnp.where(keep, a3 * a_sh, a3)
            d *= 2
        car = hc_s[:, l0:l1]
        hs = []
        for n in range(n_tile):
            hn = a3[n] * car + b3[n]
            hs.append(hn)
            car = jnp.broadcast_to(hn[SUBLANES - 1:SUBLANES, :], (SUBLANES, LANES))
        hc_s[:, l0:l1] = car
        outs.append(jnp.concatenate(hs, axis=0) * _gelu_tanh(gate[:, l0:l1]))
    return outs


def _inproj_kernel(h_ref, sh_ref, sc_ref, g_ref, w_ref, o_ref):
    u = _modnorm(h_ref[0], g_ref[...], sc_ref[0], sh_ref[0])
    o_ref[0] = _dot(u.astype(BF16), w_ref[...])


def _in_projection(h, mod_l, g, w_pad, tm=512):
    bsz, seq, d = h.shape
    npad = w_pad.shape[1]
    mod3 = mod_l.reshape(bsz, 1, N_MOD * d)
    return pl.pallas_call(
        _inproj_kernel,
        grid=(bsz, seq // tm),
        in_specs=[pl.BlockSpec((1, tm, d), lambda b, i: (b, i, 0)),
                  pl.BlockSpec((1, 1, d), lambda b, i: (b, 0, 0)),
                  pl.BlockSpec((1, 1, d), lambda b, i: (b, 0, 1)),
                  pl.BlockSpec((1, d), lambda b, i: (0, 0)),
                  pl.BlockSpec((d, npad), lambda b, i: (0, 0))],
        out_specs=pl.BlockSpec((1, tm, npad), lambda b, i: (b, i, 0)),
        out_shape=jax.ShapeDtypeStruct((bsz, seq, npad), F32),
        compiler_params=_cparams(2),
        name="norm_inproj",
    )(h, mod3, mod3, g.reshape(1, d), w_pad)


def _head_of(idx):
    return idx // GDN_DK


def _blockdiag(x):
    xb = x.astype(BF16)
    lane_head = _head_of(lax.broadcasted_iota(jnp.int32, xb.shape, 1))
    zero = jnp.zeros_like(xb)
    return jnp.concatenate([jnp.where(lane_head == hh, xb, zero) for hh in range(GDN_HEADS)], axis=0)


def _weave(*stages, start=None):
    start = start or [0] * len(stages)
    results = [None] * len(stages)
    live = list(range(len(stages)))
    turn = 0
    while live:
        for idx in list(live):
            if turn < start[idx]:
                continue
            try:
                next(stages[idx])
            except StopIteration as stop:
                results[idx] = stop.value
                live.remove(idx)
        turn += 1
    return results


def _gdn_kernel(qkvz_ref, misc_ref, convw_ref, nalog_ref, dtb_ref, ng_ref, o_ref, o_s, st_s):
    seq = qkvz_ref.shape[1]
    n_slab = seq // GDN_SLAB
    per_slab = GDN_SLAB // CHUNK
    per_part = SLAB // CHUNK

    ii = lax.broadcasted_iota(jnp.int32, (GDN_W, GDN_W), 0)
    jj = lax.broadcasted_iota(jnp.int32, (GDN_W, GDN_W), 1)
    same_blk = _head_of(ii) == _head_of(jj)
    ones_bd = jnp.where(same_blk, 1.0, 0.0).astype(BF16)
    tri_bd = jnp.where(same_blk & (ii >= jj), 1.0, 0.0).astype(BF16)
    mi = lax.broadcasted_iota(jnp.int32, (LANES, GDN_W), 0)
    mj = lax.broadcasted_iota(jnp.int32, (LANES, GDN_W), 1)
    sel_a = jnp.where(mi == MISC_A + _head_of(mj), 1.0, 0.0).astype(BF16)
    sel_b = jnp.where(mi == MISC_B + _head_of(mj), 1.0, 0.0).astype(BF16)
    ci = lax.broadcasted_iota(jnp.int32, (CHUNK, GDN_W), 0)
    cj = lax.broadcasted_iota(jnp.int32, (CHUNK, GDN_W), 1)
    cs = cj % CHUNK
    lower_incl = ci >= cs
    strict_lower = ci > cs
    eye_hl = jnp.where(ci == cs, 1.0, 0.0)
    convw = convw_ref[...]
    rng = [(c * CHUNK, (c + 1) * CHUNK) for c in range(per_slab)]

    def slab_stage(si):
        parts = []
        for p in range(GDN_SLAB // SLAB):
            parts.append((yield from part_stage(pl.multiple_of(si * GDN_SLAB + p * SLAB, SLAB))))
        rows = lambda i: jnp.concatenate([part[i] for part in parts], axis=0)
        return tuple(rows(i) for i in range(7)) + tuple(sum((part[i] for part in parts), []) for i in (7, 8))

    def part_stage(r0):
        y = _silu(_causal_conv(qkvz_ref, r0, SLAB, 0, 3 * GDN_W, convw))
        q, k, v = y[:, :GDN_W], y[:, GDN_W:2 * GDN_W], y[:, 2 * GDN_W:]

        def l2n(x):
            xx = x * x
            hi = xx.astype(BF16)
            lo = (xx - hi.astype(F32)).astype(BF16)
            ss = _dot(hi, ones_bd) + _dot(lo, ones_bd)
            return x * lax.rsqrt(ss + EPS)

        q = l2n(q) * (GDN_DK ** -0.5)
        k = l2n(k)
        yield
        misc = misc_ref[0, pl.ds(r0, SLAB), :]
        a_exp = _dot3_lhs(misc, sel_a)
        b_exp = _dot3_lhs(misc, sel_b)
        g = nalog_ref[...] * _softplus(a_exp + dtb_ref[...])
        beta = _sigmoid(b_exp)
        yield
        gcum = _dot3_rhs(tri_bd, g)
        glast = _dot3_rhs(ones_bd, g)
        eg = jnp.exp(gcum)
        kb = k * beta
        vb = (v * beta).astype(BF16)
        kbg = (kb * eg).astype(BF16)
        qd = (q * eg).astype(BF16)
        kt = k * jnp.exp(glast - gcum)
        cdec = jnp.exp(glast)
        ktts = []
        for a, b in rng[:per_part]:
            ktc = jnp.concatenate([kt[a:b], jnp.zeros((LANES - CHUNK, GDN_W), F32)], axis=0)
            ktts.append(ktc.T.astype(BF16))
        yield
        return (q.astype(BF16), k.astype(BF16), kb.astype(BF16), vb, kbg, qd, gcum, ktts,
                [cdec[a:a + 1] for a, _ in rng[:per_part]])

    def chunk_stage(arrs):
        q, k, kb, vb, kbg, qd, gcum, ktts, cds = arrs
        aqs = [_dot_nt(jnp.concatenate([kb[a:b], q[a:b]], axis=0), _blockdiag(k[a:b])) for a, b in rng]
        yield

        g_t = [gcum[:, :LANES].T, gcum[:, LANES:].T]
        head_rows = [g_t[hh // 2][(hh % 2) * GDN_DK:(hh % 2) * GDN_DK + 1, :] for hh in range(GDN_HEADS)]
        grs = [jnp.concatenate([row[:, a:b] for row in head_rows], axis=1) for a, b in rng]
        decays = [jnp.exp(jnp.where(lower_incl, gcum[a:b] - gr, 0.0)) for (a, b), gr in zip(rng, grs)]
        yield
        lmats = [jnp.where(strict_lower, aq[:CHUNK] * dec, 0.0) for aq, dec in zip(aqs, decays)]
        qks = [jnp.where(lower_incl, aq[CHUNK:] * dec, 0.0).astype(BF16) for aq, dec in zip(aqs, decays)]
        dinvs = [eye_hl - jnp.where(ci // 2 == cs // 2, lm, 0.0) for lm in lmats]
        blk = 2
        while blk < CHUNK:
            off_diag = (ci // (2 * blk) == cs // (2 * blk)) & (ci // blk != cs // blk)
            ps = [_dot(d.astype(BF16), _blockdiag(jnp.where(off_diag, lm, 0.0))) for d, lm in zip(dinvs, lmats)]
            yield
            dinvs = [d - _dot(p.astype(BF16), _blockdiag(d)) for d, p in zip(dinvs, ps)]
            yield
            blk *= 2
        ops = []
        for (a, b), d, qk, ktt, cd in zip(rng, dinvs, qks, ktts, cds):
            tb = d.astype(BF16)
            u = _dot(tb, _blockdiag(vb[a:b]))
            w = _dot(tb, _blockdiag(kbg[a:b])).astype(BF16)
            ops.append((u, w, qd[a:b], qk, ktt, cd))
        return ops

    def recur(ops, r_base):
        for c, (u, w, qd, qk, ktt, cd) in enumerate(ops):
            wq = _dot(jnp.concatenate([w, qd], axis=0), st_s[...].astype(BF16))
            yield
            v_new = u - wq[:CHUNK]
            o_s[pl.ds(r_base + c * CHUNK, CHUNK), :] = wq[CHUNK:] + _dot(qk, _blockdiag(v_new))
            vpad = jnp.concatenate([v_new.astype(BF16), jnp.zeros((LANES - CHUNK, GDN_W), BF16)], axis=0)
            upd = _dot(ktt, vpad)
            yield
            st_s[...] = st_s[...] * cd + jnp.where(same_blk, upd, 0.0)
            yield

    def prepare(si):
        arrs = yield from slab_stage(si)
        return (yield from chunk_stage(arrs))

    def slab_body(si, prev_ops):
        r_prev = pl.multiple_of(jnp.maximum(si - 1, 0) * GDN_SLAB, GDN_SLAB)
        return _weave(recur(prev_ops, r_prev), prepare(si))[1]

    idle = [(jnp.zeros((CHUNK, GDN_W), F32), jnp.zeros((CHUNK, GDN_W), BF16), jnp.zeros((CHUNK, GDN_W), BF16),
             jnp.zeros((CHUNK, GDN_W), BF16), jnp.zeros((GDN_W, LANES), BF16), jnp.ones((1, GDN_W), F32))
            for _ in range(per_slab)]
    st_s[...] = jnp.zeros_like(st_s)
    last_ops = lax.fori_loop(0, n_slab, slab_body, idle)

    mean_bd = jnp.where(same_blk, 1.0 / GDN_DK, 0.0).astype(BF16)

    def gated_norm(r0):
        o = o_s[pl.ds(r0, SLAB), :]
        oo = o * o
        hi = oo.astype(BF16)
        lo = (oo - hi.astype(F32)).astype(BF16)
        ms = _dot(hi, mean_bd) + _dot(lo, mean_bd)
        y = o * lax.rsqrt(ms + EPS) * ng_ref[...]
        z = qkvz_ref[0, pl.ds(r0, SLAB), 3 * GDN_W:4 * GDN_W]
        o_ref[0, pl.ds(r0, SLAB), :] = (y * _silu(z)).astype(o_ref.dtype)

    last_r0 = (n_slab - 1) * GDN_SLAB

    def finished_slabs():
        for r0 in range(0, last_r0, SLAB):
            gated_norm(r0)
            yield

    _weave(recur(last_ops, last_r0), finished_slabs())
    for r0 in range(last_r0, seq, SLAB):
        gated_norm(r0)


def _gdn_mixer(proj, conv_w, a_log, dt_bias, norm_g):
    bsz, seq, _ = proj.shape
    nalog = jnp.repeat(-jnp.exp(a_log.astype(F32)), GDN_DK).reshape(1, GDN_W)
    dtb = jnp.repeat(dt_bias.astype(F32), GDN_DK).reshape(1, GDN_W)
    ng = jnp.tile(norm_g.astype(F32), GDN_HEADS).reshape(1, GDN_W)
    row = lambda shape: pl.BlockSpec(shape, lambda b: (0,) * len(shape))
    return pl.pallas_call(
        _gdn_kernel,
        grid=(bsz,),
        in_specs=[pl.BlockSpec((1, seq, 4 * GDN_W), lambda b: (b, 0, COL_GDN // (4 * GDN_W))),
                  pl.BlockSpec((1, seq, LANES), lambda b: (b, 0, COL_MISC // LANES)),
                  row((GDN_CONV, 3 * GDN_W)), row((1, GDN_W)), row((1, GDN_W)), row((1, GDN_W))],
        out_specs=pl.BlockSpec((1, seq, GDN_W), lambda b: (b, 0, 0)),
        out_shape=jax.ShapeDtypeStruct((bsz, seq, GDN_W), BF16),
        scratch_shapes=[pltpu.VMEM((seq, GDN_W), F32),
                        pltpu.VMEM((GDN_W, GDN_W), F32)],
        compiler_params=_cparams(1),
        name="gdn_mixer",
    )(proj, proj, conv_w, nalog, dtb, ng)


def _rg_weights(w_a, w_x):
    def pair_bd(w):
        z = jnp.zeros((RG_BLOCK, RG_BLOCK), w.dtype)
        out = []
        for gi in range(RG_WIDTH // LANES):
            top = jnp.concatenate([w[2 * gi], z], axis=1)
            bot = jnp.concatenate([z, w[2 * gi + 1]], axis=1)
            out.append(jnp.concatenate([top, bot], axis=0))
        return jnp.stack(out)
    return jnp.concatenate([pair_bd(w_a), pair_bd(w_x)], axis=2).astype(BF16)


MLA_TQ = SLAB
PE0 = MLA_NOPE
PE_HALF = MLA_ROPE // 2


def _rms(x, g):
    ms = jnp.mean(x * x, axis=-1, keepdims=True)
    return x * lax.rsqrt(ms + EPS) * g


def _mla_kernel(mq_ref, mkv_ref, misc_ref, pos_ref, invf_ref, qg_ref, kvg_ref, wqt_ref, wk_ref, wvt_ref,
                o_ref, qt_s, k_s, vt_s):
    seq = mq_ref.shape[1]
    n_slab = seq // SLAB
    n_qblk = seq // MLA_TQ
    scale = (MLA_NOPE + MLA_ROPE) ** -0.5
    x1_0, x2_0, pe_end = PE0, PE0 + PE_HALF, PE0 + MLA_ROPE

    def prep_body(si, carry):
        r0 = pl.multiple_of(si * SLAB, SLAB)
        ang_t = invf_ref[...] * pos_ref[0, si].astype(F32)
        cos_t = jnp.cos(ang_t)
        sin_t = jnp.sin(ang_t)

        def rope_t(x1, x2):
            return x1 * cos_t - x2 * sin_t, x2 * cos_t + x1 * sin_t

        qn = _rms(mq_ref[0, pl.ds(r0, SLAB), :], qg_ref[...]).astype(BF16)
        q_t = _dot_nt(wqt_ref[...], qn)
        for hh in range(MLA_HEADS):
            blk = q_t[hh * HEAD_PAD:(hh + 1) * HEAD_PAD]
            x1, x2 = rope_t(blk[x1_0:x2_0], blk[x2_0:pe_end])
            roped = jnp.concatenate([blk[:PE0], x1, x2, blk[pe_end:]], axis=0)
            qt_s[si, hh * HEAD_PAD:(hh + 1) * HEAD_PAD, :] = (roped * scale).astype(BF16)
        kvn = _rms(mkv_ref[0, pl.ds(r0, SLAB), :], kvg_ref[...]).astype(BF16)
        kn = _dot(kvn, wk_ref[...])
        vt_s[si] = _dot_nt(wvt_ref[...], kvn).astype(BF16)
        misc_t = misc_ref[0, pl.ds(r0, SLAB), :].T
        k1, k2 = rope_t(misc_t[:PE_HALF], misc_t[PE_HALF:MLA_ROPE])
        kr = jnp.concatenate([jnp.zeros((PE0, SLAB), F32), k1, k2,
                              jnp.zeros((HEAD_PAD - pe_end, SLAB), F32)], axis=0).T
        for hh in range(MLA_HEADS):
            l0, l1 = hh * HEAD_PAD, (hh + 1) * HEAD_PAD
            k_s[pl.ds(r0, SLAB), l0:l1] = (kn[:, l0:l1] + kr).astype(BF16)
        return carry

    lax.fori_loop(0, n_slab, prep_body, 0)

    ki = lax.broadcasted_iota(jnp.int32, (MLA_TQ, MLA_TQ), 0)
    qi_ = lax.broadcasted_iota(jnp.int32, (MLA_TQ, MLA_TQ), 1)
    diag_mask = (ki // CHUNK) <= (qi_ // CHUNK)

    def q_body(qi, carry):
        q0 = pl.multiple_of(qi * MLA_TQ, MLA_TQ)

        def score(kj):
            k0 = pl.multiple_of(kj * MLA_TQ, MLA_TQ)
            return tuple(_dot(k_s[pl.ds(k0, MLA_TQ), hh * HEAD_PAD:(hh + 1) * HEAD_PAD],
                              qt_s[qi, hh * HEAD_PAD:(hh + 1) * HEAD_PAD, :])
                         for hh in range(MLA_HEADS))

        def attend(kjs, state, masked):
            scores = [score(kj) for kj in kjs]
            stats, probs = [], []
            for hh in range(MLA_HEADS):
                m, l, _ = state[hh]
                tiles = [sc[hh] for sc in scores]
                if masked:
                    tiles = [jnp.where(diag_mask, s, -jnp.inf) for s in tiles]
                m_new = m
                for s in tiles:
                    m_new = jnp.maximum(m_new, jnp.max(s, axis=0, keepdims=True))
                alpha = jnp.exp(m - m_new)
                ps = [jnp.exp(s - m_new) for s in tiles]
                l = alpha * l
                for p in ps:
                    l = l + jnp.sum(p, axis=0, keepdims=True)
                stats.append((m_new, l, alpha))
                probs.append([p.astype(BF16) for p in ps])
            new = []
            for hh, (m_new, l, alpha) in enumerate(stats):
                acc = alpha * state[hh][2]
                for kj, p in zip(kjs, probs[hh]):
                    acc = acc + _dot(vt_s[kj, hh * MLA_V:(hh + 1) * MLA_V, :], p)
                new.append((m_new, l, acc))
            return tuple(new)

        init = tuple((jnp.full((1, MLA_TQ), -jnp.inf, F32), jnp.zeros((1, MLA_TQ), F32),
                      jnp.zeros((MLA_V, MLA_TQ), F32)) for _ in range(MLA_HEADS))
        state = attend([qi], init, True)
        state = lax.cond(qi % 2 == 1, lambda st: attend([qi - 1], st, False), lambda st: st, state)
        state = lax.fori_loop(0, qi // 2, lambda kp, st: attend([2 * kp, 2 * kp + 1], st, False), state)
        out_t = jnp.concatenate([acc / l for (_, l, acc) in state], axis=0)
        o_ref[0, pl.ds(q0, MLA_TQ), :] = out_t.T.astype(o_ref.dtype)
        return carry

    lax.fori_loop(0, n_qblk, q_body, 0)


def _mla_weights(w_qb, w_kvb):
    qr = w_qb.reshape(MLA_Q_RANK, MLA_HEADS, MLA_NOPE + MLA_ROPE)
    zq = jnp.zeros((MLA_Q_RANK, MLA_HEADS, HEAD_PAD - MLA_NOPE - MLA_ROPE), w_qb.dtype)
    wq = jnp.concatenate([qr, zq], axis=2).reshape(MLA_Q_RANK, MLA_HEADS * HEAD_PAD)
    kvr = w_kvb.reshape(MLA_KV_RANK, MLA_HEADS, MLA_NOPE + MLA_V)
    zk = jnp.zeros((MLA_KV_RANK, MLA_HEADS, HEAD_PAD - MLA_NOPE), w_kvb.dtype)
    wk = jnp.concatenate([kvr[:, :, :MLA_NOPE], zk], axis=2).reshape(MLA_KV_RANK, MLA_HEADS * HEAD_PAD)
    wvt = kvr[:, :, MLA_NOPE:].reshape(MLA_KV_RANK, MLA_HEADS * MLA_V).T
    return wq.T.astype(BF16), wk.astype(BF16), wvt.astype(BF16)


def _mla_mixer(proj, positions, q_norm_g, w_qb, kv_norm_g, w_kvb):
    bsz, seq, _ = proj.shape
    inv_freq = ROPE_THETA ** (-jnp.arange(0, MLA_ROPE, 2, dtype=F32) / MLA_ROPE)
    invf = inv_freq.reshape(PE_HALF, 1)
    wqt, wk, wvt = _mla_weights(w_qb, w_kvb)
    wide = MLA_HEADS * HEAD_PAD
    vw = MLA_HEADS * MLA_V
    row = lambda shape: pl.BlockSpec(shape, lambda b: (0,) * len(shape))
    return pl.pallas_call(
        _mla_kernel,
        grid=(bsz,),
        in_specs=[pl.BlockSpec((1, seq, MLA_Q_RANK), lambda b: (b, 0, COL_MQ // MLA_Q_RANK)),
                  pl.BlockSpec((1, seq, MLA_KV_RANK), lambda b: (b, 0, COL_MKV // MLA_KV_RANK)),
                  pl.BlockSpec((1, seq, LANES), lambda b: (b, 0, COL_MISC // LANES)),
                  pl.BlockSpec((1, seq // SLAB, 1, SLAB), lambda b: (b, 0, 0, 0)),
                  row((PE_HALF, 1)), row((1, MLA_Q_RANK)), row((1, MLA_KV_RANK)),
                  row((wide, MLA_Q_RANK)), row((MLA_KV_RANK, wide)), row((vw, MLA_KV_RANK))],
        out_specs=pl.BlockSpec((1, seq, vw), lambda b: (b, 0, 0)),
        out_shape=jax.ShapeDtypeStruct((bsz, seq, vw), BF16),
        scratch_shapes=[pltpu.VMEM((seq // SLAB, wide, SLAB), BF16), pltpu.VMEM((seq, wide), BF16),
                        pltpu.VMEM((seq // SLAB, vw, SLAB), BF16)],
        compiler_params=_cparams(1),
        name="mla_mixer",
    )(proj, proj, proj, positions.reshape(bsz, seq // SLAB, 1, SLAB), invf,
      q_norm_g.astype(F32).reshape(1, MLA_Q_RANK), kv_norm_g.astype(F32).reshape(1, MLA_KV_RANK), wqt, wk, wvt)


FF_TILE = 1024


def _order_after(x, anchor):
    zero_bits = lax.shift_right_logical(pltpu.bitcast(anchor, jnp.uint32), jnp.uint32(32))
    return x + pltpu.bitcast(zero_bits, F32)


def _rg_front(rx, tail, convw_ref, convb_ref, wrg_ref, xc_s, pre_s):
    convw = convw_ref[...]
    for s0 in range(0, rx.shape[0], SLAB):
        prev = tail if s0 == 0 else rx[s0 - SUBLANES:s0]
        xc = _conv_taps(prev, rx[s0:s0 + SLAB], convw) + convb_ref[...]
        xc_s[s0:s0 + SLAB, :] = xc
        for gi, pre in enumerate(_rg_gates(xc, wrg_ref)):
            pre_s[s0:s0 + SLAB, gi * 2 * LANES:(gi + 1) * 2 * LANES] = pre


def _rg_back(xc_s, pre_s, gate, anchor, ba_ref, bx_ref, lam_ref, hc_s, ob_s):
    for s0 in range(0, xc_s.shape[0], SLAB):
        pres = [pre_s[s0:s0 + SLAB, gi * 2 * LANES:(gi + 1) * 2 * LANES] for gi in range(RG_WIDTH // LANES)]
        if anchor is not None:
            pres = [_order_after(p, anchor) for p in pres]
        outs = _rg_scan(xc_s[s0:s0 + SLAB, :], pres, gate[s0:s0 + SLAB], ba_ref, bx_ref, lam_ref, hc_s)
        for gi, o in enumerate(outs):
            ob_s[s0:s0 + SLAB, gi * LANES:(gi + 1) * LANES] = o.astype(ob_s.dtype)


def _outmlp_kernel(h_ref, oa_ref, oc_ref, gtm_ref, shf_ref, scf_ref, gtf_ref, g_ref, fg_ref,
                   wo_ref, w1_ref, w2_ref,
                   rx0_ref, gate0_ref, rxn_ref, gaten_ref, tailn_ref,
                   convw_ref, convb_ref, wrg_ref, ba_ref, bx_ref, lam_ref,
                   o_ref, ob_s, hc_s, xc_s, pre_s, *, final_norm):
    n_i = pl.num_programs(1)
    step = pl.program_id(0) * n_i + pl.program_id(1)
    front_refs = (convw_ref, convb_ref, wrg_ref, xc_s, pre_s)
    back_refs = (ba_ref, bx_ref, lam_ref, hc_s, ob_s)

    @pl.when(step == 0)
    def _():
        hc_s[...] = jnp.zeros_like(hc_s)
        _rg_front(rx0_ref[0], jnp.zeros((SUBLANES, RG_WIDTH), F32), *front_refs)
        _rg_back(xc_s, pre_s, gate0_ref[0], None, *back_refs)

    seq_head = (step + 1) % n_i == 0
    hc_s[...] = jnp.where(seq_head, 0.0, hc_s[...])
    _rg_front(rxn_ref[0], jnp.where(seq_head, 0.0, tailn_ref[0]), *front_refs)
    na, nb = oa_ref.shape[2], ob_s.shape[1]
    mix = (_dot(oa_ref[0], wo_ref[0:na, :]) + _dot(ob_s[...], wo_ref[na:na + nb, :])
           + _dot(oc_ref[0], wo_ref[na + nb:, :]))
    h1 = h_ref[0] + gtm_ref[0] * mix
    u = _modnorm(h1, g_ref[...], scf_ref[0], shf_ref[0]).astype(BF16)
    d_ff = w1_ref.shape[1]
    f = jnp.zeros_like(h1)
    anchor = None
    for t in range(d_ff // FF_TILE):
        hid = jnp.maximum(_dot(u, w1_ref[:, t * FF_TILE:(t + 1) * FF_TILE]), 0.0)
        if t == 0:
            anchor = hid[0:1, 0:2 * LANES]
        f = f + _dot((hid * hid).astype(BF16), w2_ref[t * FF_TILE:(t + 1) * FF_TILE, :])
    h2 = h1 + gtf_ref[0] * f
    if final_norm:
        ms = jnp.mean(h2 * h2, axis=-1, keepdims=True)
        h2 = h2 * lax.rsqrt(ms + EPS) * fg_ref[...]
    o_ref[0] = h2
    _rg_back(xc_s, pre_s, gaten_ref[0], anchor, *back_refs)


def _out_mlp(h, o_a, o_c, proj, mod_l, g, final_g, w_out, w1, w2, rg_params, final_norm, tm=512):
    bsz, seq, d = h.shape
    d_ff = w1.shape[1]
    n_i = seq // tm
    last = bsz * n_i - 1
    mod3 = mod_l.reshape(bsz, 1, N_MOD * d)
    conv_w, conv_b, w_a, b_a, w_x, b_x, lam = rg_params
    r2 = lambda v: v.astype(F32).reshape(1, RG_WIDTH)
    modspec = lambda m: pl.BlockSpec((1, 1, d), lambda b, i: (b, 0, m))
    tile = lambda n: pl.BlockSpec((1, tm, n), lambda b, i: (b, i, 0))
    const = lambda shape: pl.BlockSpec(shape, lambda b, i: (0,) * len(shape), pipeline_mode=pl.Buffered(1))

    def nxt(b, i):
        lin = jnp.minimum(b * n_i + i + 1, last)
        return lin // n_i, lin % n_i

    def next_tile(col):
        return pl.BlockSpec((1, tm, RG_WIDTH), lambda b, i: (*nxt(b, i), col // RG_WIDTH))

    def next_tail(b, i):
        nb, ni = nxt(b, i)
        return nb, jnp.maximum(ni * (tm // SUBLANES) - 1, 0), COL_RX // RG_WIDTH

    first_tile = lambda col: pl.BlockSpec((1, tm, RG_WIDTH), lambda b, i: (0, 0, col // RG_WIDTH))
    return pl.pallas_call(
        functools.partial(_outmlp_kernel, final_norm=final_norm),
        grid=(bsz, n_i),
        in_specs=[tile(d), tile(o_a.shape[2]), tile(o_c.shape[2]),
                  modspec(2), modspec(3), modspec(4), modspec(5),
                  const((1, d)), const((1, d)),
                  const((d, d)), const((d, d_ff)), const((d_ff, d)),
                  first_tile(COL_RX), first_tile(COL_RGATE), next_tile(COL_RX), next_tile(COL_RGATE),
                  pl.BlockSpec((1, SUBLANES, RG_WIDTH), next_tail),
                  const((RG_CONV, RG_WIDTH)), const((1, RG_WIDTH)),
                  const((RG_WIDTH // LANES, LANES, 2 * LANES)),
                  const((1, RG_WIDTH)), const((1, RG_WIDTH)), const((1, RG_WIDTH))],
        out_specs=tile(d),
        out_shape=jax.ShapeDtypeStruct((bsz, seq, d), F32),
        scratch_shapes=[pltpu.VMEM((tm, RG_WIDTH), BF16),
                        pltpu.VMEM((SUBLANES, RG_WIDTH), F32),
                        pltpu.VMEM((tm, RG_WIDTH), F32),
                        pltpu.VMEM((tm, 2 * RG_WIDTH), F32)],
        compiler_params=_cparams(2),
        name="outproj_mlp_rglru",
    )(h, o_a, o_c, mod3, mod3, mod3, mod3, g.reshape(1, d), final_g.reshape(1, d), w_out, w1, w2,
      proj, proj, proj, proj, proj,
      conv_w, r2(conv_b), _rg_weights(w_a, w_x), r2(b_a), r2(b_x), r2(lam))


def _pad_w_in(w_in_l):
    d = w_in_l.shape[0]
    o = 0
    parts = {}
    for name, n in (("gqkvz", 4 * GDN_W), ("ga", GDN_HEADS), ("gb", GDN_HEADS), ("rx", RG_WIDTH),
                    ("rgate", RG_WIDTH), ("mq", MLA_Q_RANK), ("mkv", MLA_KV_RANK), ("mkr", MLA_ROPE)):
        parts[name] = w_in_l[:, o:o + n]
        o += n
    pad = jnp.zeros((d, NP_IN - COL_MISC - MLA_ROPE - 2 * GDN_HEADS), w_in_l.dtype)
    return jnp.concatenate([parts["gqkvz"], parts["rx"], parts["rgate"], parts["mq"], parts["mkv"],
                            parts["mkr"], parts["ga"], parts["gb"], pad], axis=1).astype(BF16)


def kernel(x, c, positions, w_mod, b_mod, norm_mix_g, w_in, gdn_conv_w, gdn_a_log, gdn_dt_bias, gdn_norm_g, rg_conv_w, rg_conv_b, rg_w_a, rg_b_a, rg_w_x, rg_b_x, rg_lambda, mla_q_norm_g, mla_w_qb, mla_kv_norm_g, mla_w_kvb, w_out, norm_mlp_g, w_mlp_in, w_mlp_out, final_norm_g):
    depth = w_mod.shape[0]
    mod = _modulation(c, w_mod, b_mod)
    h = x
    for l in range(depth):
        rg_params = (rg_conv_w[l], rg_conv_b[l], rg_w_a[l], rg_b_a[l], rg_w_x[l], rg_b_x[l], rg_lambda[l])
        proj = _in_projection(h, mod[l], norm_mix_g[l], _pad_w_in(w_in[l]))
        o_a = _gdn_mixer(proj, gdn_conv_w[l], gdn_a_log[l], gdn_dt_bias[l], gdn_norm_g[l])
        o_c = _mla_mixer(proj, positions, mla_q_norm_g[l], mla_w_qb[l], mla_kv_norm_g[l], mla_w_kvb[l])
        h = _out_mlp(h, o_a, o_c, proj, mod[l], norm_mlp_g[l], final_norm_g,
                     w_out[l].astype(BF16), w_mlp_in[l].astype(BF16), w_mlp_out[l].astype(BF16),
                     rg_params, final_norm=(l == depth - 1))
    return h
```

```python
import functools

import jax
import jax.numpy as jnp
from jax import lax
from jax.experimental import pallas as pl
from jax.experimental.pallas import tpu as pltpu

F32 = jnp.float32
BF16 = jnp.bfloat16

EPS = 1e-6
N_MOD = 6
CHUNK = 64
GDN_HEADS = 4
GDN_DK = 64
GDN_W = GDN_HEADS * GDN_DK
GDN_CONV = 4
RG_WIDTH = 512
RG_BLOCK = 64
RG_CONV = 4
RG_C = 8.0
MLA_HEADS = 4
MLA_NOPE = 64
MLA_ROPE = 32
MLA_V = 64
MLA_Q_RANK = 256
MLA_KV_RANK = 128
ROPE_THETA = 10000.0
LANES = 128
SUBLANES = 8
HEAD_PAD = LANES

COL_GDN = 0
COL_RX = 1024
COL_RGATE = 1536
COL_MQ = 2048
COL_MKV = 2304
COL_MISC = 2432
NP_IN = 2560
MISC_A = MLA_ROPE
MISC_B = MLA_ROPE + GDN_HEADS

SLAB = 256
GDN_SLAB = 512
VMEM_LIMIT = 56 * 1024 * 1024


def _cparams(n_axes):
    return pltpu.CompilerParams(dimension_semantics=("arbitrary",) * n_axes,
                                vmem_limit_bytes=VMEM_LIMIT)


def _dot(a, b):
    return jnp.dot(a, b, preferred_element_type=F32)


def _dot_nt(a, b):
    return lax.dot_general(a, b, (((1,), (1,)), ((), ())), preferred_element_type=F32)


def _split3(x):
    hi = x.astype(BF16)
    r1 = x - hi.astype(F32)
    mid = r1.astype(BF16)
    lo = (r1 - mid.astype(F32)).astype(BF16)
    return hi, mid, lo


def _dot3_lhs(x, w):
    hi, mid, lo = _split3(x)
    return _dot(hi, w) + _dot(mid, w) + _dot(lo, w)


def _dot3_rhs(w, x):
    hi, mid, lo = _split3(x)
    return _dot(w, hi) + _dot(w, mid) + _dot(w, lo)


def _sigmoid(x):
    return 1.0 / (1.0 + jnp.exp(-x))


def _silu(x):
    return x * _sigmoid(x)


def _softplus(x):
    return jnp.maximum(x, 0.0) + jnp.log1p(jnp.exp(-jnp.abs(x)))


def _conv_taps(prev, cur, w):
    taps = w.shape[0]
    xcat = jnp.concatenate([prev, cur], axis=0)
    acc = cur * w[taps - 1:taps, :]
    for j in range(taps - 1):
        shifted = pltpu.roll(xcat, taps - 1 - j, axis=0)[SUBLANES:, :]
        acc = acc + shifted * w[j:j + 1, :]
    return acc


def _causal_conv(ref, r0, rows, col0, ncols, w):
    cur = ref[0, pl.ds(r0, rows), col0:col0 + ncols]
    pstart = pl.multiple_of(jnp.maximum(r0 - SUBLANES, 0), SUBLANES)
    prev = ref[0, pl.ds(pstart, SUBLANES), col0:col0 + ncols]
    return _conv_taps(jnp.where(r0 > 0, prev, 0.0), cur, w)


def _mod_kernel(c_ref, w_ref, b_ref, o_ref):
    ca = _silu(c_ref[...])
    o_ref[0] = _dot(ca.astype(BF16), w_ref[0].astype(BF16)) + b_ref[0]


def _modulation(c, w_mod, b_mod):
    depth, d, n = w_mod.shape
    bsz = c.shape[0]
    tn = 1536
    return pl.pallas_call(
        _mod_kernel,
        grid=(depth, n // tn),
        in_specs=[pl.BlockSpec((bsz, d), lambda l, j: (0, 0)),
                  pl.BlockSpec((1, d, tn), lambda l, j: (l, 0, j)),
                  pl.BlockSpec((1, 1, tn), lambda l, j: (l, 0, j))],
        out_specs=pl.BlockSpec((1, bsz, tn), lambda l, j: (l, 0, j)),
        out_shape=jax.ShapeDtypeStruct((depth, bsz, n), F32),
        compiler_params=_cparams(2),
        name="adaln_mod",
    )(c, w_mod, b_mod.reshape(depth, 1, n))


def _modnorm(x, g, sc, sh):
    ms = jnp.mean(x * x, axis=-1, keepdims=True)
    return (x * lax.rsqrt(ms + EPS) * g) * (1.0 + sc) + sh


def _gelu_tanh(x):
    return 0.5 * x * (1.0 + jnp.tanh(0.7978845608028654 * (x + 0.044715 * (x * x * x))))


def _rg_gates(xc, w_ref):
    return [_dot(xc[:, gi * LANES:(gi + 1) * LANES].astype(BF16), w_ref[gi]) for gi in range(RG_WIDTH // LANES)]


def _rg_scan(xc_all, pres, gate, ba_ref, bx_ref, lam_ref, hc_s):
    n_tile = SLAB // SUBLANES
    sub = lax.broadcasted_iota(jnp.int32, (n_tile, SUBLANES, LANES), 1)
    outs = []
    for gi, pre in enumerate(pres):
        l0, l1 = gi * LANES, (gi + 1) * LANES
        if outs:
            done = outs[-1][0:1, :]
            pre = _order_after(pre, jnp.concatenate([done, done], axis=1))
        xc = xc_all[:, l0:l1]
        r = _sigmoid(pre[:, :LANES] + ba_ref[:, l0:l1])
        ig = _sigmoid(pre[:, LANES:] + bx_ref[:, l0:l1])
        log_a = (-RG_C) * r * _softplus(-lam_ref[:, l0:l1])
        a = jnp.exp(log_a)
        om = 1.0 - a * a
        mult = jnp.where(om > 0.0, om * lax.rsqrt(om), 0.0)
        bt = mult * (ig * xc)
        a3 = a.reshape(n_tile, SUBLANES, LANES)
        b3 = bt.reshape(n_tile, SUBLANES, LANES)
        d = 1
        while d < SUBLANES:
            a_sh = pltpu.roll(a3, d, axis=1)
            b_sh = pltpu.roll(b3, d, axis=1)
            keep = sub >= d
            b3 = jnp.where(keep, a3 * b_sh + b3, b3)
            a3 = jnp.where(keep, a3 * a_sh, a3)
            d *= 2
        car = hc_s[:, l0:l1]
        hs = []
        for n in range(n_tile):
            hn = a3[n] * car + b3[n]
            hs.append(hn)
            car = jnp.broadcast_to(hn[SUBLANES - 1:SUBLANES, :], (SUBLANES, LANES))
        hc_s[:, l0:l1] = car
        outs.append(jnp.concatenate(hs, axis=0) * _gelu_tanh(gate[:, l0:l1]))
    return outs


def _inproj_kernel(h_ref, sh_ref, sc_ref, g_ref, w_ref, o_ref):
    u = _modnorm(h_ref[0], g_ref[...], sc_ref[0], sh_ref[0])
    o_ref[0] = _dot(u.astype(BF16), w_ref[0])


def _in_projection(h, mod_l, g, w_pad, layer, tm=512):
    bsz, seq, d = h.shape
    npad = w_pad.shape[2]
    mod3 = mod_l.reshape(bsz, 1, N_MOD * d)
    return pl.pallas_call(
        _inproj_kernel,
        grid=(bsz, seq // tm),
        in_specs=[pl.BlockSpec((1, tm, d), lambda b, i: (b, i, 0)),
                  pl.BlockSpec((1, 1, d), lambda b, i: (b, 0, 0)),
                  pl.BlockSpec((1, 1, d), lambda b, i: (b, 0, 1)),
                  pl.BlockSpec((1, d), lambda b, i: (0, 0)),
                  pl.BlockSpec((1, d, npad), lambda b, i: (layer, 0, 0))],
        out_specs=pl.BlockSpec((1, tm, npad), lambda b, i: (b, i, 0)),
        out_shape=jax.ShapeDtypeStruct((bsz, seq, npad), F32),
        compiler_params=_cparams(2),
        name="norm_inproj",
    )(h, mod3, mod3, g.reshape(1, d), w_pad)


def _head_of(idx):
    return idx // GDN_DK


def _blockdiag(x):
    xb = x.astype(BF16)
    lane_head = _head_of(lax.broadcasted_iota(jnp.int32, xb.shape, 1))
    zero = jnp.zeros_like(xb)
    return jnp.concatenate([jnp.where(lane_head == hh, xb, zero) for hh in range(GDN_HEADS)], axis=0)


def _weave(*stages, start=None):
    start = start or [0] * len(stages)
    results = [None] * len(stages)
    live = list(range(len(stages)))
    turn = 0
    while live:
        for idx in list(live):
            if turn < start[idx]:
                continue
            try:
                next(stages[idx])
            except StopIteration as stop:
                results[idx] = stop.value
                live.remove(idx)
        turn += 1
    return results


def _gdn_kernel(qkvz_ref, misc_ref, convw_ref, nalog_ref, dtb_ref, ng_ref, o_ref, o_s, st_s):
    seq = qkvz_ref.shape[1]
    n_slab = seq // GDN_SLAB
    per_slab = GDN_SLAB // CHUNK
    per_part = SLAB // CHUNK

    ii = lax.broadcasted_iota(jnp.int32, (GDN_W, GDN_W), 0)
    jj = lax.broadcasted_iota(jnp.int32, (GDN_W, GDN_W), 1)
    same_blk = _head_of(ii) == _head_of(jj)
    ones_bd = jnp.where(same_blk, 1.0, 0.0).astype(BF16)
    tri_bd = jnp.where(same_blk & (ii >= jj), 1.0, 0.0).astype(BF16)
    mi = lax.broadcasted_iota(jnp.int32, (LANES, GDN_W), 0)
    mj = lax.broadcasted_iota(jnp.int32, (LANES, GDN_W), 1)
    sel_a = jnp.where(mi == MISC_A + _head_of(mj), 1.0, 0.0).astype(BF16)
    sel_b = jnp.where(mi == MISC_B + _head_of(mj), 1.0, 0.0).astype(BF16)
    ci = lax.broadcasted_iota(jnp.int32, (CHUNK, GDN_W), 0)
    cj = lax.broadcasted_iota(jnp.int32, (CHUNK, GDN_W), 1)
    cs = cj % CHUNK
    lower_incl = ci >= cs
    strict_lower = ci > cs
    eye_hl = jnp.where(ci == cs, 1.0, 0.0)
    convw = convw_ref[...]
    rng = [(c * CHUNK, (c + 1) * CHUNK) for c in range(per_slab)]

    def slab_stage(si):
        parts = []
        for p in range(GDN_SLAB // SLAB):
            parts.append((yield from part_stage(pl.multiple_of(si * GDN_SLAB + p * SLAB, SLAB))))
        rows = lambda i: jnp.concatenate([part[i] for part in parts], axis=0)
        return tuple(rows(i) for i in range(7)) + tuple(sum((part[i] for part in parts), []) for i in (7, 8))

    def part_stage(r0):
        y = _silu(_causal_conv(qkvz_ref, r0, SLAB, 0, 3 * GDN_W, convw))
        q, k, v = y[:, :GDN_W], y[:, GDN_W:2 * GDN_W], y[:, 2 * GDN_W:]

        def l2n(x):
            xx = x * x
            hi = xx.astype(BF16)
            lo = (xx - hi.astype(F32)).astype(BF16)
            ss = _dot(hi, ones_bd) + _dot(lo, ones_bd)
            return x * lax.rsqrt(ss + EPS)

        q = l2n(q) * (GDN_DK ** -0.5)
        k = l2n(k)
        yield
        misc = misc_ref[0, pl.ds(r0, SLAB), :]
        a_exp = _dot3_lhs(misc, sel_a)
        b_exp = _dot3_lhs(misc, sel_b)
        g = nalog_ref[...] * _softplus(a_exp + dtb_ref[...])
        beta = _sigmoid(b_exp)
        yield
        gcum = _dot3_rhs(tri_bd, g)
        glast = _dot3_rhs(ones_bd, g)
        eg = jnp.exp(gcum)
        kb = k * beta
        vb = (v * beta).astype(BF16)
        kbg = (kb * eg).astype(BF16)
        qd = (q * eg).astype(BF16)
        kt = k * jnp.exp(glast - gcum)
        cdec = jnp.exp(glast)
        ktts = []
        for a, b in rng[:per_part]:
            ktc = jnp.concatenate([kt[a:b], jnp.zeros((LANES - CHUNK, GDN_W), F32)], axis=0)
            ktts.append(ktc.T.astype(BF16))
        yield
        return (q.astype(BF16), k.astype(BF16), kb.astype(BF16), vb, kbg, qd, gcum, ktts,
                [cdec[a:a + 1] for a, _ in rng[:per_part]])

    def chunk_stage(arrs):
        q, k, kb, vb, kbg, qd, gcum, ktts, cds = arrs
        aqs = [_dot_nt(jnp.concatenate([kb[a:b], q[a:b]], axis=0), _blockdiag(k[a:b])) for a, b in rng]
        yield

        g_t = [gcum[:, :LANES].T, gcum[:, LANES:].T]
        head_rows = [g_t[hh // 2][(hh % 2) * GDN_DK:(hh % 2) * GDN_DK + 1, :] for hh in range(GDN_HEADS)]
        grs = [jnp.concatenate([row[:, a:b] for row in head_rows], axis=1) for a, b in rng]
        decays = [jnp.exp(jnp.where(lower_incl, gcum[a:b] - gr, 0.0)) for (a, b), gr in zip(rng, grs)]
        yield
        lmats = [jnp.where(strict_lower, aq[:CHUNK] * dec, 0.0) for aq, dec in zip(aqs, decays)]
        qks = [jnp.where(lower_incl, aq[CHUNK:] * dec, 0.0).astype(BF16) for aq, dec in zip(aqs, decays)]
        dinvs = [eye_hl - jnp.where(ci // 2 == cs // 2, lm, 0.0) for lm in lmats]
        blk = 2
        while blk < CHUNK:
            off_diag = (ci // (2 * blk) == cs // (2 * blk)) & (ci // blk != cs // blk)
            ps = [_dot(d.astype(BF16), _blockdiag(jnp.where(off_diag, lm, 0.0))) for d, lm in zip(dinvs, lmats)]
            yield
            dinvs = [d - _dot(p.astype(BF16), _blockdiag(d)) for d, p in zip(dinvs, ps)]
            yield
            blk *= 2
        ops = []
        for (a, b), d, qk, ktt, cd in zip(rng, dinvs, qks, ktts, cds):
            tb = d.astype(BF16)
            u = _dot(tb, _blockdiag(vb[a:b]))
            w = _dot(tb, _blockdiag(kbg[a:b])).astype(BF16)
            ops.append((u, w, qd[a:b], qk, ktt, cd))
        return ops

    def recur(ops, r_base):
        for c, (u, w, qd, qk, ktt, cd) in enumerate(ops):
            wq = _dot(jnp.concatenate([w, qd], axis=0), st_s[...].astype(BF16))
            yield
            v_new = u - wq[:CHUNK]
            o_s[pl.ds(r_base + c * CHUNK, CHUNK), :] = wq[CHUNK:] + _dot(qk, _blockdiag(v_new))
            vpad = jnp.concatenate([v_new.astype(BF16), jnp.zeros((LANES - CHUNK, GDN_W), BF16)], axis=0)
            upd = _dot(ktt, vpad)
            yield
            st_s[...] = st_s[...] * cd + jnp.where(same_blk, upd, 0.0)
            yield

    def prepare(si):
        arrs = yield from slab_stage(si)
        return (yield from chunk_stage(arrs))

    def slab_body(si, prev_ops):
        r_prev = pl.multiple_of(jnp.maximum(si - 1, 0) * GDN_SLAB, GDN_SLAB)
        return _weave(recur(prev_ops, r_prev), prepare(si))[1]

    idle = [(jnp.zeros((CHUNK, GDN_W), F32), jnp.zeros((CHUNK, GDN_W), BF16), jnp.zeros((CHUNK, GDN_W), BF16),
             jnp.zeros((CHUNK, GDN_W), BF16), jnp.zeros((GDN_W, LANES), BF16), jnp.ones((1, GDN_W), F32))
            for _ in range(per_slab)]
    st_s[...] = jnp.zeros_like(st_s)
    last_ops = lax.fori_loop(0, n_slab, slab_body, idle)

    mean_bd = jnp.where(same_blk, 1.0 / GDN_DK, 0.0).astype(BF16)

    def gated_norm(r0):
        o = o_s[pl.ds(r0, SLAB), :]
        oo = o * o
        hi = oo.astype(BF16)
        lo = (oo - hi.astype(F32)).astype(BF16)
        ms = _dot(hi, mean_bd) + _dot(lo, mean_bd)
        y = o * lax.rsqrt(ms + EPS) * ng_ref[...]
        z = qkvz_ref[0, pl.ds(r0, SLAB), 3 * GDN_W:4 * GDN_W]
        o_ref[0, pl.ds(r0, SLAB), :] = (y * _silu(z)).astype(o_ref.dtype)

    last_r0 = (n_slab - 1) * GDN_SLAB

    def finished_slabs():
        for r0 in range(0, last_r0, SLAB):
            gated_norm(r0)
            yield

    _weave(recur(last_ops, last_r0), finished_slabs())
    for r0 in range(last_r0, seq, SLAB):
        gated_norm(r0)


def _gdn_mixer(proj, conv_w, a_log, dt_bias, norm_g):
    bsz, seq, _ = proj.shape
    nalog = jnp.repeat(-jnp.exp(a_log.astype(F32)), GDN_DK).reshape(1, GDN_W)
    dtb = jnp.repeat(dt_bias.astype(F32), GDN_DK).reshape(1, GDN_W)
    ng = jnp.tile(norm_g.astype(F32), GDN_HEADS).reshape(1, GDN_W)
    row = lambda shape: pl.BlockSpec(shape, lambda b: (0,) * len(shape))
    return pl.pallas_call(
        _gdn_kernel,
        grid=(bsz,),
        in_specs=[pl.BlockSpec((1, seq, 4 * GDN_W), lambda b: (b, 0, COL_GDN // (4 * GDN_W))),
                  pl.BlockSpec((1, seq, LANES), lambda b: (b, 0, COL_MISC // LANES)),
                  row((GDN_CONV, 3 * GDN_W)), row((1, GDN_W)), row((1, GDN_W)), row((1, GDN_W))],
        out_specs=pl.BlockSpec((1, seq, GDN_W), lambda b: (b, 0, 0)),
        out_shape=jax.ShapeDtypeStruct((bsz, seq, GDN_W), BF16),
        scratch_shapes=[pltpu.VMEM((seq, GDN_W), F32),
                        pltpu.VMEM((GDN_W, GDN_W), F32)],
        compiler_params=_cparams(1),
        name="gdn_mixer",
    )(proj, proj, conv_w, nalog, dtb, ng)


def _rg_weights(w_a, w_x):
    def pair_bd(w):
        z = jnp.zeros((RG_BLOCK, RG_BLOCK), w.dtype)
        out = []
        for gi in range(RG_WIDTH // LANES):
            top = jnp.concatenate([w[2 * gi], z], axis=1)
            bot = jnp.concatenate([z, w[2 * gi + 1]], axis=1)
            out.append(jnp.concatenate([top, bot], axis=0))
        return jnp.stack(out)
    return jnp.concatenate([pair_bd(w_a), pair_bd(w_x)], axis=2).astype(BF16)


MLA_TQ = SLAB
PE0 = MLA_NOPE
PE_HALF = MLA_ROPE // 2


def _rms(x, g):
    ms = jnp.mean(x * x, axis=-1, keepdims=True)
    return x * lax.rsqrt(ms + EPS) * g


def _mla_kernel(mq_ref, mkv_ref, misc_ref, pos_ref, invf_ref, qg_ref, kvg_ref, wqt_ref, wk_ref, wvt_ref,
                o_ref, qt_s, k_s, vt_s):
    seq = mq_ref.shape[1]
    n_slab = seq // SLAB
    n_qblk = seq // MLA_TQ
    scale = (MLA_NOPE + MLA_ROPE) ** -0.5
    x1_0, x2_0, pe_end = PE0, PE0 + PE_HALF, PE0 + MLA_ROPE

    def prep_body(si, carry):
        r0 = pl.multiple_of(si * SLAB, SLAB)
        ang_t = invf_ref[...] * pos_ref[0, si].astype(F32)
        cos_t = jnp.cos(ang_t)
        sin_t = jnp.sin(ang_t)

        def rope_t(x1, x2):
            return x1 * cos_t - x2 * sin_t, x2 * cos_t + x1 * sin_t

        qn = _rms(mq_ref[0, pl.ds(r0, SLAB), :], qg_ref[...]).astype(BF16)
        q_t = _dot_nt(wqt_ref[...], qn)
        for hh in range(MLA_HEADS):
            blk = q_t[hh * HEAD_PAD:(hh + 1) * HEAD_PAD]
            x1, x2 = rope_t(blk[x1_0:x2_0], blk[x2_0:pe_end])
            roped = jnp.concatenate([blk[:PE0], x1, x2, blk[pe_end:]], axis=0)
            qt_s[si, hh * HEAD_PAD:(hh + 1) * HEAD_PAD, :] = (roped * scale).astype(BF16)
        kvn = _rms(mkv_ref[0, pl.ds(r0, SLAB), :], kvg_ref[...]).astype(BF16)
        kn = _dot(kvn, wk_ref[...])
        vt_s[si] = _dot_nt(wvt_ref[...], kvn).astype(BF16)
        misc_t = misc_ref[0, pl.ds(r0, SLAB), :].T
        k1, k2 = rope_t(misc_t[:PE_HALF], misc_t[PE_HALF:MLA_ROPE])
        kr = jnp.concatenate([jnp.zeros((PE0, SLAB), F32), k1, k2,
                              jnp.zeros((HEAD_PAD - pe_end, SLAB), F32)], axis=0).T
        for hh in range(MLA_HEADS):
            l0, l1 = hh * HEAD_PAD, (hh + 1) * HEAD_PAD
            k_s[pl.ds(r0, SLAB), l0:l1] = (kn[:, l0:l1] + kr).astype(BF16)
        return carry

    lax.fori_loop(0, n_slab, prep_body, 0)

    ki = lax.broadcasted_iota(jnp.int32, (MLA_TQ, MLA_TQ), 0)
    qi_ = lax.broadcasted_iota(jnp.int32, (MLA_TQ, MLA_TQ), 1)
    diag_mask = (ki // CHUNK) <= (qi_ // CHUNK)

    def q_body(qi, carry):
        q0 = pl.multiple_of(qi * MLA_TQ, MLA_TQ)

        def score(kj):
            k0 = pl.multiple_of(kj * MLA_TQ, MLA_TQ)
            return tuple(_dot(k_s[pl.ds(k0, MLA_TQ), hh * HEAD_PAD:(hh + 1) * HEAD_PAD],
                              qt_s[qi, hh * HEAD_PAD:(hh + 1) * HEAD_PAD, :])
                         for hh in range(MLA_HEADS))

        def attend(kjs, state, masked):
            scores = [score(kj) for kj in kjs]
            stats, probs = [], []
            for hh in range(MLA_HEADS):
                m, l, _ = state[hh]
                tiles = [sc[hh] for sc in scores]
                if masked:
                    tiles = [jnp.where(diag_mask, s, -jnp.inf) for s in tiles]
                m_new = m
                for s in tiles:
                    m_new = jnp.maximum(m_new, jnp.max(s, axis=0, keepdims=True))
                alpha = jnp.exp(m - m_new)
                ps = [jnp.exp(s - m_new) for s in tiles]
                l = alpha * l
                for p in ps:
                    l = l + jnp.sum(p, axis=0, keepdims=True)
                stats.append((m_new, l, alpha))
                probs.append([p.astype(BF16) for p in ps])
            new = []
            for hh, (m_new, l, alpha) in enumerate(stats):
                acc = alpha * state[hh][2]
                for kj, p in zip(kjs, probs[hh]):
                    acc = acc + _dot(vt_s[kj, hh * MLA_V:(hh + 1) * MLA_V, :], p)
                new.append((m_new, l, acc))
            return tuple(new)

        init = tuple((jnp.full((1, MLA_TQ), -jnp.inf, F32), jnp.zeros((1, MLA_TQ), F32),
                      jnp.zeros((MLA_V, MLA_TQ), F32)) for _ in range(MLA_HEADS))
        state = attend([qi], init, True)
        state = lax.cond(qi % 2 == 1, lambda st: attend([qi - 1], st, False), lambda st: st, state)
        state = lax.fori_loop(0, qi // 2, lambda kp, st: attend([2 * kp, 2 * kp + 1], st, False), state)
        out_t = jnp.concatenate([acc / l for (_, l, acc) in state], axis=0)
        o_ref[0, pl.ds(q0, MLA_TQ), :] = out_t.T.astype(o_ref.dtype)
        return carry

    lax.fori_loop(0, n_qblk, q_body, 0)


def _mla_weights(w_qb, w_kvb):
    qr = w_qb.reshape(MLA_Q_RANK, MLA_HEADS, MLA_NOPE + MLA_ROPE)
    zq = jnp.zeros((MLA_Q_RANK, MLA_HEADS, HEAD_PAD - MLA_NOPE - MLA_ROPE), w_qb.dtype)
    wq = jnp.concatenate([qr, zq], axis=2).reshape(MLA_Q_RANK, MLA_HEADS * HEAD_PAD)
    kvr = w_kvb.reshape(MLA_KV_RANK, MLA_HEADS, MLA_NOPE + MLA_V)
    zk = jnp.zeros((MLA_KV_RANK, MLA_HEADS, HEAD_PAD - MLA_NOPE), w_kvb.dtype)
    wk = jnp.concatenate([kvr[:, :, :MLA_NOPE], zk], axis=2).reshape(MLA_KV_RANK, MLA_HEADS * HEAD_PAD)
    wvt = kvr[:, :, MLA_NOPE:].reshape(MLA_KV_RANK, MLA_HEADS * MLA_V).T
    return wq.T.astype(BF16), wk.astype(BF16), wvt.astype(BF16)


def _mla_mixer(proj, positions, q_norm_g, w_qb, kv_norm_g, w_kvb):
    bsz, seq, _ = proj.shape
    inv_freq = ROPE_THETA ** (-jnp.arange(0, MLA_ROPE, 2, dtype=F32) / MLA_ROPE)
    invf = inv_freq.reshape(PE_HALF, 1)
    wqt, wk, wvt = _mla_weights(w_qb, w_kvb)
    wide = MLA_HEADS * HEAD_PAD
    vw = MLA_HEADS * MLA_V
    row = lambda shape: pl.BlockSpec(shape, lambda b: (0,) * len(shape))
    return pl.pallas_call(
        _mla_kernel,
        grid=(bsz,),
        in_specs=[pl.BlockSpec((1, seq, MLA_Q_RANK), lambda b: (b, 0, COL_MQ // MLA_Q_RANK)),
                  pl.BlockSpec((1, seq, MLA_KV_RANK), lambda b: (b, 0, COL_MKV // MLA_KV_RANK)),
                  pl.BlockSpec((1, seq, LANES), lambda b: (b, 0, COL_MISC // LANES)),
                  pl.BlockSpec((1, seq // SLAB, 1, SLAB), lambda b: (b, 0, 0, 0)),
                  row((PE_HALF, 1)), row((1, MLA_Q_RANK)), row((1, MLA_KV_RANK)),
                  row((wide, MLA_Q_RANK)), row((MLA_KV_RANK, wide)), row((vw, MLA_KV_RANK))],
        out_specs=pl.BlockSpec((1, seq, vw), lambda b: (b, 0, 0)),
        out_shape=jax.ShapeDtypeStruct((bsz, seq, vw), BF16),
        scratch_shapes=[pltpu.VMEM((seq // SLAB, wide, SLAB), BF16), pltpu.VMEM((seq, wide), BF16),
                        pltpu.VMEM((seq // SLAB, vw, SLAB), BF16)],
        compiler_params=_cparams(1),
        name="mla_mixer",
    )(proj, proj, proj, positions.reshape(bsz, seq // SLAB, 1, SLAB), invf,
      q_norm_g.astype(F32).reshape(1, MLA_Q_RANK), kv_norm_g.astype(F32).reshape(1, MLA_KV_RANK), wqt, wk, wvt)


FF_TILE = 1024


def _order_after(x, anchor):
    zero_bits = lax.shift_right_logical(pltpu.bitcast(anchor, jnp.uint32), jnp.uint32(32))
    return x + pltpu.bitcast(zero_bits, F32)


def _rg_front(rx, tail, convw_ref, convb_ref, wrg_ref, xc_s, pre_s):
    convw = convw_ref[...]
    for s0 in range(0, rx.shape[0], SLAB):
        prev = tail if s0 == 0 else rx[s0 - SUBLANES:s0]
        xc = _conv_taps(prev, rx[s0:s0 + SLAB], convw) + convb_ref[...]
        xc_s[s0:s0 + SLAB, :] = xc
        for gi, pre in enumerate(_rg_gates(xc, wrg_ref)):
            pre_s[s0:s0 + SLAB, gi * 2 * LANES:(gi + 1) * 2 * LANES] = pre


def _rg_back(xc_s, pre_s, gate, anchor, ba_ref, bx_ref, lam_ref, hc_s, ob_s):
    for s0 in range(0, xc_s.shape[0], SLAB):
        pres = [pre_s[s0:s0 + SLAB, gi * 2 * LANES:(gi + 1) * 2 * LANES] for gi in range(RG_WIDTH // LANES)]
        if anchor is not None:
            pres = [_order_after(p, anchor) for p in pres]
        outs = _rg_scan(xc_s[s0:s0 + SLAB, :], pres, gate[s0:s0 + SLAB], ba_ref, bx_ref, lam_ref, hc_s)
        for gi, o in enumerate(outs):
            ob_s[s0:s0 + SLAB, gi * LANES:(gi + 1) * LANES] = o.astype(ob_s.dtype)


def _outmlp_kernel(h_ref, oa_ref, oc_ref, gtm_ref, shf_ref, scf_ref, gtf_ref, g_ref, fg_ref,
                   wo_ref, w1_ref, w2_ref,
                   rx0_ref, gate0_ref, rxn_ref, gaten_ref, tailn_ref,
                   convw_ref, convb_ref, wrg_ref, ba_ref, bx_ref, lam_ref,
                   o_ref, ob_s, hc_s, xc_s, pre_s, *, final_norm):
    n_i = pl.num_programs(1)
    step = pl.program_id(0) * n_i + pl.program_id(1)
    front_refs = (convw_ref, convb_ref, wrg_ref, xc_s, pre_s)
    back_refs = (ba_ref, bx_ref, lam_ref, hc_s, ob_s)

    @pl.when(step == 0)
    def _():
        hc_s[...] = jnp.zeros_like(hc_s)
        _rg_front(rx0_ref[0], jnp.zeros((SUBLANES, RG_WIDTH), F32), *front_refs)
        _rg_back(xc_s, pre_s, gate0_ref[0], None, *back_refs)

    seq_head = (step + 1) % n_i == 0
    hc_s[...] = jnp.where(seq_head, 0.0, hc_s[...])
    _rg_front(rxn_ref[0], jnp.where(seq_head, 0.0, tailn_ref[0]), *front_refs)
    na, nb = oa_ref.shape[2], ob_s.shape[1]
    mix = (_dot(oa_ref[0], wo_ref[0, 0:na, :]) + _dot(ob_s[...], wo_ref[0, na:na + nb, :])
           + _dot(oc_ref[0], wo_ref[0, na + nb:, :]))
    h1 = h_ref[0] + gtm_ref[0] * mix
    u = _modnorm(h1, g_ref[...], scf_ref[0], shf_ref[0]).astype(BF16)
    d_ff = w1_ref.shape[2]
    f = jnp.zeros_like(h1)
    anchor = None
    for t in range(d_ff // FF_TILE):
        hid = jnp.maximum(_dot(u, w1_ref[0, :, t * FF_TILE:(t + 1) * FF_TILE]), 0.0)
        if t == 0:
            anchor = hid[0:1, 0:2 * LANES]
        f = f + _dot((hid * hid).astype(BF16), w2_ref[0, t * FF_TILE:(t + 1) * FF_TILE, :])
    h2 = h1 + gtf_ref[0] * f
    if final_norm:
        ms = jnp.mean(h2 * h2, axis=-1, keepdims=True)
        h2 = h2 * lax.rsqrt(ms + EPS) * fg_ref[...]
    o_ref[0] = h2
    _rg_back(xc_s, pre_s, gaten_ref[0], anchor, *back_refs)


def _out_mlp(h, o_a, o_c, proj, mod_l, g, final_g, w_out, w1, w2, layer, rg_params, final_norm, tm=512):
    bsz, seq, d = h.shape
    d_ff = w1.shape[2]
    n_i = seq // tm
    last = bsz * n_i - 1
    mod3 = mod_l.reshape(bsz, 1, N_MOD * d)
    conv_w, conv_b, w_a, b_a, w_x, b_x, lam = rg_params
    r2 = lambda v: v.astype(F32).reshape(1, RG_WIDTH)
    modspec = lambda m: pl.BlockSpec((1, 1, d), lambda b, i: (b, 0, m))
    tile = lambda n: pl.BlockSpec((1, tm, n), lambda b, i: (b, i, 0))
    const = lambda shape: pl.BlockSpec(shape, lambda b, i: (0,) * len(shape), pipeline_mode=pl.Buffered(1))
    of_layer = lambda shape: pl.BlockSpec((1,) + shape, lambda b, i: (layer, 0, 0), pipeline_mode=pl.Buffered(1))

    def nxt(b, i):
        lin = jnp.minimum(b * n_i + i + 1, last)
        return lin // n_i, lin % n_i

    def next_tile(col):
        return pl.BlockSpec((1, tm, RG_WIDTH), lambda b, i: (*nxt(b, i), col // RG_WIDTH))

    def next_tail(b, i):
        nb, ni = nxt(b, i)
        return nb, jnp.maximum(ni * (tm // SUBLANES) - 1, 0), COL_RX // RG_WIDTH

    first_tile = lambda col: pl.BlockSpec((1, tm, RG_WIDTH), lambda b, i: (0, 0, col // RG_WIDTH))
    return pl.pallas_call(
        functools.partial(_outmlp_kernel, final_norm=final_norm),
        grid=(bsz, n_i),
        in_specs=[tile(d), tile(o_a.shape[2]), tile(o_c.shape[2]),
                  modspec(2), modspec(3), modspec(4), modspec(5),
                  const((1, d)), const((1, d)),
                  of_layer((d, d)), of_layer((d, d_ff)), of_layer((d_ff, d)),
                  first_tile(COL_RX), first_tile(COL_RGATE), next_tile(COL_RX), next_tile(COL_RGATE),
                  pl.BlockSpec((1, SUBLANES, RG_WIDTH), next_tail),
                  const((RG_CONV, RG_WIDTH)), const((1, RG_WIDTH)),
                  const((RG_WIDTH // LANES, LANES, 2 * LANES)),
                  const((1, RG_WIDTH)), const((1, RG_WIDTH)), const((1, RG_WIDTH))],
        out_specs=tile(d),
        out_shape=jax.ShapeDtypeStruct((bsz, seq, d), F32),
        scratch_shapes=[pltpu.VMEM((tm, RG_WIDTH), BF16),
                        pltpu.VMEM((SUBLANES, RG_WIDTH), F32),
                        pltpu.VMEM((tm, RG_WIDTH), F32),
                        pltpu.VMEM((tm, 2 * RG_WIDTH), F32)],
        compiler_params=_cparams(2),
        name="outproj_mlp_rglru",
    )(h, o_a, o_c, mod3, mod3, mod3, mod3, g.reshape(1, d), final_g.reshape(1, d), w_out, w1, w2,
      proj, proj, proj, proj, proj,
      conv_w, r2(conv_b), _rg_weights(w_a, w_x), r2(b_a), r2(b_x), r2(lam))


def _pad_w_in(w_in):
    o = 0
    parts = {}
    for name, n in (("gqkvz", 4 * GDN_W), ("ga", GDN_HEADS), ("gb", GDN_HEADS), ("rx", RG_WIDTH),
                    ("rgate", RG_WIDTH), ("mq", MLA_Q_RANK), ("mkv", MLA_KV_RANK), ("mkr", MLA_ROPE)):
        parts[name] = w_in[..., o:o + n]
        o += n
    pad = jnp.zeros(w_in.shape[:-1] + (NP_IN - COL_MISC - MLA_ROPE - 2 * GDN_HEADS,), w_in.dtype)
    return jnp.concatenate([parts["gqkvz"], parts["rx"], parts["rgate"], parts["mq"], parts["mkv"],
                            parts["mkr"], parts["ga"], parts["gb"], pad], axis=-1).astype(BF16)


def kernel(x, c, positions, w_mod, b_mod, norm_mix_g, w_in, gdn_conv_w, gdn_a_log, gdn_dt_bias, gdn_norm_g, rg_conv_w, rg_conv_b, rg_w_a, rg_b_a, rg_w_x, rg_b_x, rg_lambda, mla_q_norm_g, mla_w_qb, mla_kv_norm_g, mla_w_kvb, w_out, norm_mlp_g, w_mlp_in, w_mlp_out, final_norm_g):
    depth = w_mod.shape[0]
    mod = _modulation(c, w_mod, b_mod)
    w_in_b, w_out_b = _pad_w_in(w_in), w_out.astype(BF16)
    w1_b, w2_b = w_mlp_in.astype(BF16), w_mlp_out.astype(BF16)
    h = x
    for l in range(depth):
        rg_params = (rg_conv_w[l], rg_conv_b[l], rg_w_a[l], rg_b_a[l], rg_w_x[l], rg_b_x[l], rg_lambda[l])
        proj = _in_projection(h, mod[l], norm_mix_g[l], w_in_b, l)
        o_a = _gdn_mixer(proj, gdn_conv_w[l], gdn_a_log[l], gdn_dt_bias[l], gdn_norm_g[l])
        o_c = _mla_mixer(proj, positions, mla_q_norm_g[l], mla_w_qb[l], mla_kv_norm_g[l], mla_w_kvb[l])
        h = _out_mlp(h, o_a, o_c, proj, mod[l], norm_mlp_g[l], final_norm_g, w_out_b, w1_b, w2_b, l,
                     rg_params, final_norm=(l == depth - 1))
    return h
```
